```python
import jax
import jax.numpy as jnp
from jax import lax
import numpy as np

D_MODEL = 1024
BATCH = 16
SEQ = 256
DEPTH = 2
DEC_BATCH = 2
DEC_SEQ = 4096
PAST_LEN = 512

GRID_W = 64
HEAD_DIM = 64
WA_HEADS = 8
WA_KV = 2
AX_HEADS = 8
AX_KV = 2
WINDOW = 128
Q_BLOCK = 128
LRU_WIDTH = 512
LRU_BLOCKS = 8
LRU_BW = LRU_WIDTH // LRU_BLOCKS
CONV_W = 4
LRU_C = 8.0
PEER_HEADS = 8
PEER_NKEYS = 128
PEER_EXPERTS = PEER_NKEYS * PEER_NKEYS
PEER_DKEY = 128
PEER_TOPK = 16
TOKEN_BLOCK = 128
ROPE_BASE = 10000.0
EPS = 1e-6
NEG = -1e30
WA_QW = WA_HEADS * HEAD_DIM
WA_KVW = WA_KV * HEAD_DIM
AX_QW = AX_HEADS * HEAD_DIM
AX_KVW = AX_KV * HEAD_DIM
IN_SPLITS = (WA_QW, WA_KVW, WA_KVW, AX_QW, AX_KVW, AX_KVW, LRU_WIDTH, LRU_WIDTH, D_MODEL, D_MODEL, D_MODEL)
IN_WIDTH = WA_QW + 2 * WA_KVW + AX_QW + 2 * AX_KVW + 2 * LRU_WIDTH + 3 * D_MODEL

kernel_name = "hybrid_gated_mixers_peer_ctx_prefix"

F32 = jnp.float32


def rmsnorm(x, g):
    xf = x.astype(F32)
    y = xf * lax.rsqrt(jnp.mean(xf * xf, axis=-1, keepdims=True) + EPS)
    return (y * g.astype(F32)).astype(x.dtype)


def modulate(x, g, shift, scale):
    return rmsnorm(x, g) * (1 + scale) + shift


def ada_params(cvec, w_ada, b_ada):
    m = jnp.einsum('bd,de->be', jax.nn.silu(cvec), w_ada) + b_ada
    return jnp.split(m[:, None, :], 6, axis=-1)


def split_cols(z):
    out = []
    start = 0
    for w in IN_SPLITS:
        out.append(z[..., start:start + w])
        start += w
    return out


def axial_rope(T):
    n_rows = T // GRID_W
    rows = jnp.repeat(jnp.arange(n_rows), GRID_W).astype(F32)
    cols = jnp.tile(jnp.arange(GRID_W), n_rows).astype(F32)
    n_freq = HEAD_DIM // 4
    inv = ROPE_BASE ** (-jnp.arange(n_freq, dtype=F32) / n_freq)
    ang = jnp.concatenate([rows[:, None] * inv, cols[:, None] * inv], axis=-1)
    return jnp.cos(ang), jnp.sin(ang)


def apply_rope(x, cos, sin):
    half = HEAD_DIM // 2
    shape = (1, x.shape[1]) + (1,) * (x.ndim - 3) + (half,)
    c = cos.reshape(shape)
    s = sin.reshape(shape)
    xf = x.astype(F32)
    x1, x2 = xf[..., :half], xf[..., half:]
    return jnp.concatenate([x1 * c - x2 * s, x2 * c + x1 * s], axis=-1).astype(x.dtype)


def softmax_with_sink(s, sink):
    if sink is None:
        return jax.nn.softmax(s, axis=-1)
    sk = sink.astype(F32)[None, :, :, None, None]
    m = jnp.maximum(jnp.max(s, axis=-1, keepdims=True), sk)
    e = jnp.exp(s - m)
    return e / (jnp.sum(e, axis=-1, keepdims=True) + jnp.exp(sk - m))


def block_attention(q, k, v, sink):
    B, T, KV, G, Dh = q.shape
    nb = T // Q_BLOCK
    qb = jnp.moveaxis(q.reshape(B, nb, Q_BLOCK, KV, G, Dh), 1, 0)
    scale = HEAD_DIM ** -0.5

    def one(qi):
        s = jnp.einsum('bqhgd,bkhd->bhgqk', qi, k, preferred_element_type=F32) * scale
        p = softmax_with_sink(s, sink).astype(v.dtype)
        return jnp.einsum('bhgqk,bkhd->bqhgd', p, v)

    o = lax.map(one, qb)
    return jnp.moveaxis(o, 0, 1).reshape(B, T, KV * G * Dh)


def window_attention_ctx(q, k, v, kc, vc, sink):
    B, T, KV, G, Dh = q.shape
    Lc = kc.shape[1]
    nb = T // Q_BLOCK
    span = Q_BLOCK + 2 * WINDOW
    kp = jnp.pad(k, ((0, 0), (WINDOW, WINDOW), (0, 0), (0, 0)))
    vp = jnp.pad(v, ((0, 0), (WINDOW, WINDOW), (0, 0), (0, 0)))
    qb = jnp.moveaxis(q.reshape(B, nb, Q_BLOCK, KV, G, Dh), 1, 0)
    a_idx = jnp.arange(Q_BLOCK)[:, None]
    j_idx = jnp.arange(span)[None, :]
    scale = HEAD_DIM ** -0.5

    def one(args):
        i, qi = args
        start = i * Q_BLOCK
        ki = lax.dynamic_slice_in_dim(kp, start, span, axis=1)
        vi = lax.dynamic_slice_in_dim(vp, start, span, axis=1)
        qpos = start + a_idx
        kpos = start - WINDOW + j_idx
        valid = (jnp.abs(kpos - qpos) <= WINDOW) & (kpos >= 0) & (kpos < T)
        s_loc = jnp.einsum('bqhgd,bkhd->bhgqk', qi, ki, preferred_element_type=F32) * scale
        s_loc = jnp.where(valid, s_loc, NEG)
        s_ctx = jnp.einsum('bqhgd,bkhd->bhgqk', qi, kc, preferred_element_type=F32) * scale
        p = softmax_with_sink(jnp.concatenate([s_ctx, s_loc], axis=-1), sink).astype(v.dtype)
        return (jnp.einsum('bhgqk,bkhd->bqhgd', p[..., :Lc], vc)
                + jnp.einsum('bhgqk,bkhd->bqhgd', p[..., Lc:], vi))

    o = lax.map(one, (jnp.arange(nb), qb))
    return jnp.moveaxis(o, 0, 1).reshape(B, T, KV * G * Dh)


def depthwise_conv(x, w, b):
    y = lax.conv_general_dilated(
        x, w[:, None, :].astype(x.dtype), window_strides=(1,),
        padding=[((CONV_W - 1) // 2, CONV_W // 2)],
        dimension_numbers=('NWC', 'WIO', 'NWC'), feature_group_count=x.shape[-1])
    return y + b


def block_diag(x, w):
    B, T, _ = x.shape
    y = jnp.einsum('btnc,ncd->btnd', x.reshape(B, T, LRU_BLOCKS, LRU_BW), w)
    return y.reshape(B, T, LRU_WIDTH)


def scan_combine(left, right):
    a1, b1 = left
    a2, b2 = right
    return a1 * a2, a2 * b1 + b2


def rglru(x, wa, ba, wx, bx, lam, h0, reverse):
    xf = x.astype(F32)
    r = jax.nn.sigmoid(block_diag(xf, wa.astype(F32)) + ba.astype(F32))
    i = jax.nn.sigmoid(block_diag(xf, wx.astype(F32)) + bx.astype(F32))
    log_a = -LRU_C * r * jax.nn.softplus(-lam.astype(F32))
    a = jnp.exp(log_a)
    b = jnp.sqrt(-jnp.expm1(2.0 * log_a)) * (i * xf)
    if reverse:
        a = jnp.flip(a, axis=1)
        b = jnp.flip(b, axis=1)
    b = b.at[:, 0].add(a[:, 0] * h0.astype(F32))
    _, hs = lax.associative_scan(scan_combine, (a, b), axis=1)
    h_last = hs[:, -1]
    if reverse:
        hs = jnp.flip(hs, axis=1)
    return hs, h_last


def token_mixers(h, w_in, wa_sink, ax_qg, ax_kg, conv_w, conv_b, lru_wa, lru_ba, lru_wx, lru_bx,
                 lru_lam, wo_a, wo_b, wo_c, w_out, ctx=None):
    B, T, _ = h.shape
    qa, ka, va, qb, kb, vb, xr, yr, ga, gb, gc = split_cols(jnp.einsum('btd,de->bte', h, w_in))
    ga_n = WA_HEADS // WA_KV
    gb_n = AX_HEADS // AX_KV
    qa = qa.reshape(B, T, WA_KV, ga_n, HEAD_DIM)
    ka = ka.reshape(B, T, WA_KV, HEAD_DIM)
    va = va.reshape(B, T, WA_KV, HEAD_DIM)
    qb = rmsnorm(qb.reshape(B, T, AX_KV, gb_n, HEAD_DIM), ax_qg)
    kb = rmsnorm(kb.reshape(B, T, AX_KV, HEAD_DIM), ax_kg)
    vb = vb.reshape(B, T, AX_KV, HEAD_DIM)
    sink = wa_sink.reshape(WA_KV, ga_n)
    xc = depthwise_conv(xr, conv_w, conv_b)
    if ctx is None:
        oa = block_attention(qa, ka, va, sink)
        ob = block_attention(qb, kb, vb, None)
        hf0 = jnp.zeros((B, LRU_WIDTH), F32)
        hb0 = jnp.zeros((B, LRU_WIDTH), F32)
    else:
        kca, vca, kcb, vcb, hf0, hb0 = ctx
        cos, sin = axial_rope(T)
        oa = window_attention_ctx(apply_rope(qa, cos, sin), apply_rope(ka, cos, sin), va, kca, vca, sink)
        ob = block_attention(apply_rope(qb, cos, sin),
                             jnp.concatenate([kcb, apply_rope(kb, cos, sin)], axis=1),
                             jnp.concatenate([vcb, vb], axis=1), None)
    hf, hf_last = rglru(xc, lru_wa[0], lru_ba[0], lru_wx[0], lru_bx[0], lru_lam[0], hf0, False)
    hb, hb_last = rglru(xc, lru_wa[1], lru_ba[1], lru_wx[1], lru_bx[1], lru_lam[1], hb0, True)
    oc = (hf + hb).astype(h.dtype) * jax.nn.gelu(yr)
    merged = (jax.nn.sigmoid(ga) * (oa @ wo_a)
              + jax.nn.sigmoid(gb) * (ob @ wo_b)
              + jax.nn.sigmoid(gc) * (oc @ wo_c))
    out = merged @ w_out
    if ctx is None:
        return out, (ka, va, kb, vb, hf_last.astype(h.dtype), hb_last.astype(h.dtype))
    return out, None


def peer(h, wq, keys, u, v):
    B, T, D = h.shape
    nblk = (B * T) // TOKEN_BLOCK
    xb = h.reshape(nblk, TOKEN_BLOCK, D)

    def one(xi):
        q = (xi @ wq).reshape(TOKEN_BLOCK, PEER_HEADS, 2, PEER_DKEY)
        s = jnp.einsum('thpd,hpkd->thpk', q, keys, preferred_element_type=F32)
        top_s, top_i = lax.top_k(s, PEER_TOPK)
        cand_s = top_s[:, :, 0, :, None] + top_s[:, :, 1, None, :]
        cand_i = top_i[:, :, 0, :, None] * PEER_NKEYS + top_i[:, :, 1, None, :]
        cand_s = cand_s.reshape(TOKEN_BLOCK, PEER_HEADS, PEER_TOPK * PEER_TOPK)
        cand_i = cand_i.reshape(TOKEN_BLOCK, PEER_HEADS, PEER_TOPK * PEER_TOPK)
        best_s, pos = lax.top_k(cand_s, PEER_TOPK)
        idx = jnp.take_along_axis(cand_i, pos, axis=-1)
        g = jax.nn.softmax(best_s, axis=-1)
        ue = u[idx]
        act = jax.nn.gelu(jnp.einsum('thkd,td->thk', ue, xi, preferred_element_type=F32))
        ve = v[idx]
        return jnp.einsum('thk,thkd->td', (g * act).astype(v.dtype), ve)

    y = lax.map(one, xb)
    return y.reshape(B, T, D)


def setup_inputs(seed: int = 0) -> dict:
    key = jax.random.key(seed)
    ks = jax.random.split(key, 40)

    def nrm(k, shape, s):
        return jax.random.normal(k, shape, F32) * s

    a_c = jax.random.uniform(ks[30], (DEPTH, 2, LRU_WIDTH), F32, 0.9, 0.999)
    a0 = a_c ** (1.0 / LRU_C)
    lru_lambda = jnp.log(a0) - jnp.log1p(-a0)
    return {
        "x_prompt": nrm(ks[0], (BATCH, SEQ, D_MODEL), 1.0),
        "x_sample": nrm(ks[1], (DEC_BATCH, DEC_SEQ, D_MODEL), 1.0),
        "c": nrm(ks[2], (DEC_BATCH, D_MODEL), 1.0),
        "cache_wa_k": nrm(ks[3], (DEC_BATCH, DEPTH, PAST_LEN, WA_KV, HEAD_DIM), 1.0),
        "cache_wa_v": nrm(ks[4], (DEC_BATCH, DEPTH, PAST_LEN, WA_KV, HEAD_DIM), 1.0),
        "cache_ax_k": nrm(ks[5], (DEC_BATCH, DEPTH, PAST_LEN, AX_KV, HEAD_DIM), 1.0),
        "cache_ax_v": nrm(ks[6], (DEC_BATCH, DEPTH, PAST_LEN, AX_KV, HEAD_DIM), 1.0),
        "state_lru_fwd": nrm(ks[7], (DEC_BATCH, DEPTH, LRU_WIDTH), 0.5),
        "state_lru_bwd": nrm(ks[8], (DEC_BATCH, DEPTH, LRU_WIDTH), 0.5),
        "c_ctx": nrm(ks[9], (D_MODEL,), 1.0),
        "w_ada": nrm(ks[10], (DEPTH, D_MODEL, 6 * D_MODEL), 0.5 * D_MODEL ** -0.5),
        "b_ada": nrm(ks[11], (DEPTH, 6 * D_MODEL), 0.1),
        "g_norm1": 1.0 + nrm(ks[12], (DEPTH, D_MODEL), 0.02),
        "w_in": nrm(ks[13], (DEPTH, D_MODEL, IN_WIDTH), D_MODEL ** -0.5),
        "wa_sink": nrm(ks[14], (DEPTH, WA_HEADS), 0.5),
        "ax_q_gain": 1.0 + nrm(ks[15], (DEPTH, HEAD_DIM), 0.02),
        "ax_k_gain": 1.0 + nrm(ks[16], (DEPTH, HEAD_DIM), 0.02),
        "conv_w": nrm(ks[17], (DEPTH, CONV_W, LRU_WIDTH), 0.5),
        "conv_b": nrm(ks[18], (DEPTH, LRU_WIDTH), 0.02),
        "lru_wa": nrm(ks[19], (DEPTH, 2, LRU_BLOCKS, LRU_BW, LRU_BW), LRU_BW ** -0.5),
        "lru_ba": nrm(ks[20], (DEPTH, 2, LRU_WIDTH), 0.02),
        "lru_wx": nrm(ks[21], (DEPTH, 2, LRU_BLOCKS, LRU_BW, LRU_BW), LRU_BW ** -0.5),
        "lru_bx": nrm(ks[22], (DEPTH, 2, LRU_WIDTH), 0.02),
        "lru_lambda": lru_lambda,
        "wo_a": nrm(ks[23], (DEPTH, WA_QW, D_MODEL), WA_QW ** -0.5),
        "wo_b": nrm(ks[24], (DEPTH, AX_QW, D_MODEL), AX_QW ** -0.5),
        "wo_c": nrm(ks[25], (DEPTH, LRU_WIDTH, D_MODEL), LRU_WIDTH ** -0.5),
        "w_out": nrm(ks[26], (DEPTH, D_MODEL, D_MODEL), D_MODEL ** -0.5),
        "g_norm2": 1.0 + nrm(ks[27], (DEPTH, D_MODEL), 0.02),
        "peer_wq": nrm(ks[28], (DEPTH, D_MODEL, PEER_HEADS * 2 * PEER_DKEY), D_MODEL ** -0.5),
        "peer_keys": nrm(ks[29], (DEPTH, PEER_HEADS, 2, PEER_NKEYS, PEER_DKEY), PEER_DKEY ** -0.5),
        "peer_u": nrm(ks[31], (DEPTH, PEER_EXPERTS, D_MODEL), D_MODEL ** -0.5),
        "peer_v": nrm(ks[32], (DEPTH, PEER_EXPERTS, D_MODEL), 0.5),
        "g_final": 1.0 + nrm(ks[33], (D_MODEL,), 0.02),
    }


def reference(x_prompt, x_sample, c, cache_wa_k, cache_wa_v, cache_ax_k, cache_ax_v,
              state_lru_fwd, state_lru_bwd, c_ctx, w_ada, b_ada, g_norm1, w_in, wa_sink,
              ax_q_gain, ax_k_gain, conv_w, conv_b, lru_wa, lru_ba, lru_wx, lru_bx, lru_lambda,
              wo_a, wo_b, wo_c, w_out, g_norm2, peer_wq, peer_keys, peer_u, peer_v, g_final):
    xp = x_prompt
    xs = x_sample
    wa_ks, wa_vs, ax_ks, ax_vs, hfs, hbs = [], [], [], [], [], []
    for l in range(DEPTH):
        mix_w = (w_in[l], wa_sink[l], ax_q_gain[l], ax_k_gain[l], conv_w[l], conv_b[l],
                 lru_wa[l], lru_ba[l], lru_wx[l], lru_bx[l], lru_lambda[l],
                 wo_a[l], wo_b[l], wo_c[l], w_out[l])
        sh1, sc1, gt1, sh2, sc2, gt2 = ada_params(c_ctx[None, :], w_ada[l], b_ada[l])
        o, ctx_state = token_mixers(modulate(xp, g_norm1[l], sh1, sc1), *mix_w)
        xp = xp + gt1 * o
        xp = xp + gt2 * peer(modulate(xp, g_norm2[l], sh2, sc2), peer_wq[l], peer_keys[l], peer_u[l], peer_v[l])
        ka, va, kb, vb, hf, hb = ctx_state
        wa_ks.append(ka)
        wa_vs.append(va)
        ax_ks.append(kb)
        ax_vs.append(vb)
        hfs.append(hf)
        hbs.append(hb)
        sh1, sc1, gt1, sh2, sc2, gt2 = ada_params(c, w_ada[l], b_ada[l])
        ctx = (cache_wa_k[:, l], cache_wa_v[:, l], cache_ax_k[:, l], cache_ax_v[:, l],
               state_lru_fwd[:, l], state_lru_bwd[:, l])
        o, _ = token_mixers(modulate(xs, g_norm1[l], sh1, sc1), *mix_w, ctx=ctx)
        xs = xs + gt1 * o
        xs = xs + gt2 * peer(modulate(xs, g_norm2[l], sh2, sc2), peer_wq[l], peer_keys[l], peer_u[l], peer_v[l])
    y_prompt = rmsnorm(xp, g_final)
    y_sample = rmsnorm(xs, g_final)
    new_wa_k = jnp.stack(wa_ks, axis=1)
    new_wa_v = jnp.stack(wa_vs, axis=1)
    new_ax_k = jnp.stack(ax_ks, axis=1)
    new_ax_v = jnp.stack(ax_vs, axis=1)
    new_lru_fwd = jnp.stack(hfs, axis=1)
    new_lru_bwd = jnp.stack(hbs, axis=1)
    return (y_prompt, y_sample, new_wa_k, new_wa_v, new_ax_k, new_ax_v, new_lru_fwd, new_lru_bwd)
```

```python
import functools

import jax
import jax.numpy as jnp
from jax import lax
from jax.experimental import pallas as pl
from jax.experimental.pallas import tpu as pltpu

F32 = jnp.float32
BF16 = jnp.bfloat16

HEAD_DIM = 64
KV_HEADS = 2
Q_GROUP = 4
Q_WIDTH = KV_HEADS * Q_GROUP * HEAD_DIM
KV_WIDTH = KV_HEADS * HEAD_DIM
WINDOW = 128
GRID_W = 64
LRU_WIDTH = 512
LRU_BLOCKS = 8
LRU_C = 8.0
PEER_HEADS = 8
PEER_NKEYS = 128
PEER_DKEY = 128
PEER_TOPK = 16
ROPE_BASE = 10000.0
EPS = 1e-6
NEG = -1e30
LANES = 128
SUBLANES = 8
VMEM_LIMIT = 56 * 1024 * 1024

Z1_WIDTH = 2048
Z2_WIDTH = 3584
COL_KA, COL_VA, COL_QB, COL_KB, COL_VB, COL_XR = 512, 640, 768, 1280, 1408, 1536

TOKEN_TILE = 256
ATT_A_TQ = 128
ATT_B_TQ = 256
ATT_B_TK = 512
PEER_TB = 256
PEER_EC = 2048


def _cparams(sem):
    return pltpu.CompilerParams(dimension_semantics=sem, vmem_limit_bytes=VMEM_LIMIT)


def _split(x):
    hi = x.astype(BF16)
    lo = (x - hi.astype(F32)).astype(BF16)
    return hi, lo


def _dot(a, b):
    return jnp.dot(a, b, preferred_element_type=F32)


def _dot_nt(a, b):
    return lax.dot_general(a, b, (((1,), (1,)), ((), ())), preferred_element_type=F32)


def _sigmoid(x):
    return 1.0 / (1.0 + jnp.exp(-x))


def _gelu(x):
    return 0.5 * x * (1.0 + jnp.tanh(0.7978845608028654 * (x + 0.044715 * (x * x * x))))


def _group_of_block(i, tile, nctx, dec_seq):
    row = i * tile
    return jnp.where(row < nctx, 0, 1 + (row - nctx) // dec_seq)


def _ada_kernel(c_ref, w_ref, b_ref, o_ref):
    c = c_ref[...]
    s = c * _sigmoid(c)
    s_hi, s_lo = _split(s)
    w_hi, w_lo = _split(w_ref[0])
    o_ref[0] = _dot(s_hi, w_hi) + _dot(s_lo, w_hi) + _dot(s_hi, w_lo) + b_ref[0]


def _ada(cvec, w_ada, b_ada):
    depth, d, e = w_ada.shape
    tn = 1536
    return pl.pallas_call(
        _ada_kernel,
        grid=(depth, e // tn),
        in_specs=[
            pl.BlockSpec((SUBLANES, d), lambda l, j: (0, 0)),
            pl.BlockSpec((1, d, tn), lambda l, j: (l, 0, j)),
            pl.BlockSpec((1, 1, tn), lambda l, j: (l, 0, j)),
        ],
        out_specs=pl.BlockSpec((1, SUBLANES, tn), lambda l, j: (l, 0, j)),
        out_shape=jax.ShapeDtypeStruct((depth, SUBLANES, e), F32),
        compiler_params=_cparams(("parallel", "parallel")),
        name="ada",
    )(cvec, w_ada, b_ada.reshape(depth, 1, e))


def _modmm_kernel(x_ref, mod_ref, g_ref, w_ref, *rest, shift_col, three_pass):
    d = x_ref.shape[1]
    if three_pass:
        wlo_ref, o_ref, h_scr, hlo_scr = rest
    else:
        o_ref, h_scr = rest

    @pl.when(pl.program_id(1) == 0)
    def _():
        x = x_ref[...]
        y = x * lax.rsqrt(jnp.mean(x * x, axis=-1, keepdims=True) + EPS) * g_ref[...]
        m = mod_ref[0]
        h = y * (1.0 + m[:, shift_col + d:shift_col + 2 * d]) + m[:, shift_col:shift_col + d]
        hi = h.astype(BF16)
        h_scr[...] = hi
        if three_pass:
            hlo_scr[...] = (h - hi.astype(F32)).astype(BF16)

    acc = _dot(h_scr[...], w_ref[...])
    if three_pass:
        acc = acc + _dot(hlo_scr[...], w_ref[...]) + _dot(h_scr[...], wlo_ref[...])
    o_ref[...] = acc


def _modmm(x, mod3, gain, w, w_lo, *, shift_col, tn, nctx, dec_seq):
    n, d = x.shape
    width = w.shape[1]
    tm = 512
    three_pass = w_lo is not None
    grp = lambda i, j: (_group_of_block(i, tm, nctx, dec_seq), 0, 0)
    in_specs = [
        pl.BlockSpec((tm, d), lambda i, j: (i, 0)),
        pl.BlockSpec((1, 1, mod3.shape[2]), grp),
        pl.BlockSpec((1, d), lambda i, j: (0, 0)),
        pl.BlockSpec((d, tn), lambda i, j: (0, j)),
    ]
    args = [x, mod3, gain.reshape(1, d), w]
    scratch = [pltpu.VMEM((tm, d), BF16)]
    if three_pass:
        in_specs.append(pl.BlockSpec((d, tn), lambda i, j: (0, j)))
        args.append(w_lo)
        scratch.append(pltpu.VMEM((tm, d), BF16))
    return pl.pallas_call(
        functools.partial(_modmm_kernel, shift_col=shift_col, three_pass=three_pass),
        grid=(n // tm, width // tn),
        in_specs=in_specs,
        out_specs=pl.BlockSpec((tm, tn), lambda i, j: (i, j)),
        out_shape=jax.ShapeDtypeStruct((n, width), F32),
        scratch_shapes=scratch,
        compiler_params=_cparams(("parallel", "arbitrary")),
        name="modmm",
    )(*args)


def _prep_kernel(z_ref, cos_ref, sin_ref, gq_ref, gk_ref, m_ref,
                 qsa_ref, kka_ref, vva_ref, qsb_ref, kkb_ref, vvb_ref, kbn_ref):
    tm = z_ref.shape[0]
    lane = lax.broadcasted_iota(jnp.int32, (tm, LANES), 1)
    low_head = lane < HEAD_DIM
    first_half = (lane & (HEAD_DIM // 2)) == 0
    cos = cos_ref[...]
    sin = sin_ref[...]
    mmat = m_ref[...]
    scale = HEAD_DIM ** -0.5

    def rope(x):
        back = pltpu.roll(x, HEAD_DIM // 2, 1)
        fwd = pltpu.roll(x, LANES - HEAD_DIM // 2, 1)
        return x * cos + jnp.where(first_half, fwd, back) * sin

    def headnorm(x, gain):
        sq_hi, sq_lo = _split(x * x)
        ms = _dot(sq_hi, mmat) + _dot(sq_lo, mmat)
        return x * lax.rsqrt(ms + EPS) * gain

    def store_q(ref, c, q):
        zero = jnp.zeros_like(q)
        j, g0 = c // 2, 2 * (c % 2)
        ref[j, g0] = jnp.where(low_head, q, zero).astype(BF16)
        ref[j, g0 + 1] = jnp.where(low_head, zero, q).astype(BF16)

    def store_dup(ref, x):
        swapped = pltpu.roll(x, HEAD_DIM, 1)
        ref[0] = jnp.where(low_head, x, swapped).astype(BF16)
        ref[1] = jnp.where(low_head, swapped, x).astype(BF16)

    for c in range(Q_WIDTH // LANES):
        store_q(qsa_ref, c, rope(z_ref[:, c * LANES:(c + 1) * LANES]) * scale)
        qb = headnorm(z_ref[:, COL_QB + c * LANES:COL_QB + (c + 1) * LANES], gq_ref[...])
        store_q(qsb_ref, c, rope(qb) * scale)
    store_dup(kka_ref, rope(z_ref[:, COL_KA:COL_KA + KV_WIDTH]))
    store_dup(vva_ref, z_ref[:, COL_VA:COL_VA + KV_WIDTH])
    kb = rope(headnorm(z_ref[:, COL_KB:COL_KB + KV_WIDTH], gk_ref[...]))
    kbn_ref[...] = kb
    store_dup(kkb_ref, kb)
    store_dup(vvb_ref, z_ref[:, COL_VB:COL_VB + KV_WIDTH])


def _prep(z1, cos_tab, sin_tab, gq, gk, mmat, *, nctx, dec_seq):
    n = z1.shape[0]
    tm = TOKEN_TILE
    nctx_blk = nctx // tm
    seq_blk = dec_seq // tm

    def tab_idx(i):
        return (jnp.where(i < nctx_blk, 0, 1 + (i - nctx_blk) % seq_blk), 0)

    qs_spec = pl.BlockSpec((KV_HEADS, Q_GROUP, tm, LANES), lambda i: (0, 0, i, 0))
    kv_spec = pl.BlockSpec((KV_HEADS, tm, LANES), lambda i: (0, i, 0))
    qs_shape = jax.ShapeDtypeStruct((KV_HEADS, Q_GROUP, n, LANES), BF16)
    kv_shape = jax.ShapeDtypeStruct((KV_HEADS, n, LANES), BF16)
    return pl.pallas_call(
        _prep_kernel,
        grid=(n // tm,),
        in_specs=[
            pl.BlockSpec((tm, Z1_WIDTH), lambda i: (i, 0)),
            pl.BlockSpec((tm, LANES), tab_idx),
            pl.BlockSpec((tm, LANES), tab_idx),
            pl.BlockSpec((1, LANES), lambda i: (0, 0)),
            pl.BlockSpec((1, LANES), lambda i: (0, 0)),
            pl.BlockSpec((LANES, LANES), lambda i: (0, 0)),
        ],
        out_specs=[qs_spec, kv_spec, kv_spec, qs_spec, kv_spec, kv_spec,
                   pl.BlockSpec((tm, LANES), lambda i: (i, 0))],
        out_shape=[qs_shape, kv_shape, kv_shape, qs_shape, kv_shape, kv_shape,
                   jax.ShapeDtypeStruct((n, LANES), F32)],
        compiler_params=_cparams(("parallel",)),
        name="prep",
    )(z1, cos_tab, sin_tab, gq, gk, mmat)


def _merge_heads(o, tq, c):
    lane = lax.broadcasted_iota(jnp.int32, (tq, LANES), 1)
    g = 2 * c
    return jnp.where(lane < HEAD_DIM, o[g * tq:(g + 1) * tq], o[(g + 1) * tq:(g + 2) * tq])


def _sink_column(sink_ref, j, tq):
    return jnp.concatenate(
        [jnp.full((tq, 1), sink_ref[j * Q_GROUP + g], F32) for g in range(Q_GROUP)], axis=0)


def _att_ctx_kernel(sink_ref, qsa_ref, kka_ref, vva_ref, qsb_ref, kkb_ref, vvb_ref, oa_ref, ob_ref):
    tq = qsa_ref.shape[2]
    for q_ref, k_ref, v_ref, o_ref, has_sink in (
            (qsa_ref, kka_ref, vva_ref, oa_ref, True), (qsb_ref, kkb_ref, vvb_ref, ob_ref, False)):
        for j in range(KV_HEADS):
            qs = q_ref[j].reshape(Q_GROUP * tq, LANES)
            s = _dot_nt(qs, k_ref[j])
            m = jnp.max(s, axis=-1, keepdims=True)
            if has_sink:
                sk = _sink_column(sink_ref, j, tq)
                m = jnp.maximum(m, sk)
            e = jnp.exp(s - m)
            den = jnp.sum(e, axis=-1, keepdims=True)
            if has_sink:
                den = den + jnp.exp(sk - m)
            o = _dot(e.astype(BF16), v_ref[j]) / den
            for c in range(2):
                col = (2 * j + c) * LANES
                o_ref[:, col:col + LANES] = _merge_heads(o, tq, c).astype(BF16)


def _att_ctx(sink, qsa, kka, vva, qsb, kkb, vvb, *, nseq, seq):
    qs_spec = pl.BlockSpec((KV_HEADS, Q_GROUP, seq, LANES), lambda i: (0, 0, i, 0))
    kv_spec = pl.BlockSpec((KV_HEADS, seq, LANES), lambda i: (0, i, 0))
    o_spec = pl.BlockSpec((seq, Q_WIDTH), lambda i: (i, 0))
    o_shape = jax.ShapeDtypeStruct((nseq * seq, Q_WIDTH), BF16)
    return pl.pallas_call(
        _att_ctx_kernel,
        grid=(nseq,),
        in_specs=[pl.BlockSpec(memory_space=pltpu.SMEM),
                  qs_spec, kv_spec, kv_spec, qs_spec, kv_spec, kv_spec],
        out_specs=[o_spec, o_spec],
        out_shape=[o_shape, o_shape],
        compiler_params=_cparams(("parallel",)),
        name="att_ctx",
    )(sink, qsa, kka, vva, qsb, kkb, vvb)


def _att_win_kernel(sink_ref, qs_ref, kp_ref, kc_ref, kn_ref, vp_ref, vc_ref, vn_ref,
                    kx_ref, vx_ref, o_ref, *, nblk):
    i = pl.program_id(1)
    tq = qs_ref.shape[2]
    rows = Q_GROUP * tq
    a_idx = lax.broadcasted_iota(jnp.int32, (rows, tq), 0) & (tq - 1)
    j_idx = lax.broadcasted_iota(jnp.int32, (rows, tq), 1)
    valid_prev = (j_idx >= a_idx) & (i > 0)
    valid_next = (j_idx <= a_idx) & (i < nblk - 1)
    for j in range(KV_HEADS):
        qs = qs_ref[j].reshape(rows, LANES)
        s_x = _dot_nt(qs, kx_ref[j])
        s_p = jnp.where(valid_prev, _dot_nt(qs, kp_ref[j]), NEG)
        s_c = _dot_nt(qs, kc_ref[j])
        s_n = jnp.where(valid_next, _dot_nt(qs, kn_ref[j]), NEG)
        sk = _sink_column(sink_ref, j, tq)
        m = jnp.maximum(jnp.max(s_x, axis=-1, keepdims=True), sk)
        for s in (s_p, s_c, s_n):
            m = jnp.maximum(m, jnp.max(s, axis=-1, keepdims=True))
        den = jnp.exp(sk - m)
        o = jnp.zeros((rows, LANES), F32)
        for s, v_ref in ((s_x, vx_ref), (s_p, vp_ref), (s_c, vc_ref), (s_n, vn_ref)):
            e = jnp.exp(s - m)
            den = den + jnp.sum(e, axis=-1, keepdims=True)
            o = o + _dot(e.astype(BF16), v_ref[j])
        o = o / den
        for c in range(2):
            col = (2 * j + c) * LANES
            o_ref[:, col:col + LANES] = _merge_heads(o, tq, c).astype(BF16)


def _att_win(sink, qs, kk, vv, kx, vx, *, nctx, dec_batch, dec_seq):
    tq = ATT_A_TQ
    assert tq == WINDOW
    nblk = dec_seq // tq
    base = nctx // tq
    past = kx.shape[2]

    def q_idx(b, i):
        return (0, 0, base + b * nblk + i, 0)

    def kv_idx(delta):
        return lambda b, i: (0, base + b * nblk + jnp.clip(i + delta, 0, nblk - 1), 0)

    kv_specs = [pl.BlockSpec((KV_HEADS, tq, LANES), kv_idx(d)) for d in (-1, 0, 1)]
    x_spec = pl.BlockSpec((None, KV_HEADS, past, LANES), lambda b, i: (b, 0, 0, 0))
    return pl.pallas_call(
        functools.partial(_att_win_kernel, nblk=nblk),
        grid=(dec_batch, nblk),
        in_specs=[pl.BlockSpec(memory_space=pltpu.SMEM),
                  pl.BlockSpec((KV_HEADS, Q_GROUP, tq, LANES), q_idx)] + kv_specs + kv_specs
                 + [x_spec, x_spec],
        out_specs=pl.BlockSpec((tq, Q_WIDTH), lambda b, i: (b * nblk + i, 0)),
        out_shape=jax.ShapeDtypeStruct((dec_batch * dec_seq, Q_WIDTH), BF16),
        compiler_params=_cparams(("parallel", "parallel")),
        name="att_win",
    )(sink, qs, kk, kk, kk, vv, vv, vv, kx, vx)


def _att_dense_kernel(qs_ref, k_ref, v_ref, o_ref, m_scr, l_scr, acc_scr):
    kc = pl.program_id(2)
    tq = qs_ref.shape[2]
    rows = Q_GROUP * tq

    @pl.when(kc == 0)
    def _():
        m_scr[...] = jnp.full(m_scr.shape, -jnp.inf, F32)
        l_scr[...] = jnp.zeros(l_scr.shape, F32)
        acc_scr[...] = jnp.zeros(acc_scr.shape, F32)

    for j in range(KV_HEADS):
        qs = qs_ref[j].reshape(rows, LANES)
        s = _dot_nt(qs, k_ref[j])
        m_prev = m_scr[j]
        m_new = jnp.maximum(m_prev, jnp.max(s, axis=-1, keepdims=True))
        alpha = jnp.exp(m_prev - m_new)
        p = jnp.exp(s - m_new)
        l_scr[j] = alpha * l_scr[j] + jnp.sum(p, axis=-1, keepdims=True)
        acc_scr[j] = alpha * acc_scr[j] + _dot(p.astype(BF16), v_ref[j])
        m_scr[j] = m_new

    @pl.when(kc == pl.num_programs(2) - 1)
    def _():
        for j in range(KV_HEADS):
            o = acc_scr[j] / l_scr[j]
            for c in range(2):
                col = (2 * j + c) * LANES
                o_ref[:, col:col + LANES] = _merge_heads(o, tq, c).astype(BF16)


def _att_dense(qs, k_all, v_all, *, nctx, dec_batch, dec_seq):
    tq, tk = ATT_B_TQ, ATT_B_TK
    nblk = dec_seq // tq
    base = nctx // tq
    nkeys = k_all.shape[2]
    rows = Q_GROUP * tq
    kv_spec = pl.BlockSpec((None, KV_HEADS, tk, LANES), lambda b, i, k: (b, 0, k, 0))
    return pl.pallas_call(
        _att_dense_kernel,
        grid=(dec_batch, nblk, nkeys // tk),
        in_specs=[pl.BlockSpec((KV_HEADS, Q_GROUP, tq, LANES),
                               lambda b, i, k: (0, 0, base + b * nblk + i, 0)),
                  kv_spec, kv_spec],
        out_specs=pl.BlockSpec((tq, Q_WIDTH), lambda b, i, k: (b * nblk + i, 0)),
        out_shape=jax.ShapeDtypeStruct((dec_batch * dec_seq, Q_WIDTH), BF16),
        scratch_shapes=[pltpu.VMEM((KV_HEADS, rows, 1), F32), pltpu.VMEM((KV_HEADS, rows, 1), F32),
                        pltpu.VMEM((KV_HEADS, rows, LANES), F32)],
        compiler_params=_cparams(("parallel", "parallel", "arbitrary")),
        name="att_dense",
    )(qs, k_all, v_all)


def _lru_kernel(xf_ref, xfp_ref, xfn_ref, xb_ref, xbp_ref, xbn_ref, cw_ref, cb_ref,
                wa_hi_ref, wa_lo_ref, wx_hi_ref, wx_lo_ref, ba_ref, bx_ref, lam_ref,
                h0f_ref, h0b_ref, hf_ref, hb_ref,
                af_scr, bf_scr, ab_scr, bb_scr, sf_scr, sb_scr, *, nctx_blk, seq_blk):
    i = pl.program_id(0)
    nchunk = pl.num_programs(0)
    ts = xf_ref.shape[0]
    row = lax.broadcasted_iota(jnp.int32, (ts, LRU_WIDTH), 0)
    tile_row = lax.broadcasted_iota(jnp.int32, (SUBLANES, LRU_WIDTH), 0)

    def chunk_info(c):
        lat = c >= nctx_blk
        pos = (c - nctx_blk) % seq_blk
        starts = jnp.logical_or(jnp.logical_not(lat), pos == 0)
        ends = jnp.logical_or(jnp.logical_not(lat), pos == seq_blk - 1)
        group = jnp.where(lat, 1 + (c - nctx_blk) // seq_blk, 0)
        return starts, ends, group

    def gates(x_ref, xp_ref, xn_ref, starts, ends, d, a_scr, b_scr):
        x = x_ref[...]
        prev = jnp.where(starts, 0.0, xp_ref[SUBLANES - 1:SUBLANES, :])
        nxt0 = jnp.where(ends, 0.0, xn_ref[0:1, :])
        nxt1 = jnp.where(ends, 0.0, xn_ref[1:2, :])
        x_m1 = jnp.where(row == 0, prev, pltpu.roll(x, 1, 0))
        x_p1 = jnp.where(row == ts - 1, nxt0, pltpu.roll(x, ts - 1, 0))
        x_p2 = jnp.where(row == ts - 2, nxt0, jnp.where(row == ts - 1, nxt1, pltpu.roll(x, ts - 2, 0)))
        xc = (cw_ref[0:1, :] * x_m1 + cw_ref[1:2, :] * x + cw_ref[2:3, :] * x_p1
              + cw_ref[3:4, :] * x_p2 + cb_ref[...])
        xc_hi, xc_lo = _split(xc)

        def blockdiag(hi_ref, lo_ref):
            return _dot(xc_hi, hi_ref[d]) + _dot(xc_lo, hi_ref[d]) + _dot(xc_hi, lo_ref[d])

        r = _sigmoid(blockdiag(wa_hi_ref, wa_lo_ref) + ba_ref[d])
        g = _sigmoid(blockdiag(wx_hi_ref, wx_lo_ref) + bx_ref[d])
        neg_lam = -lam_ref[d]
        softplus = jnp.maximum(neg_lam, 0.0) + jnp.log1p(jnp.exp(-jnp.abs(neg_lam)))
        log_a = -LRU_C * r * softplus
        a = jnp.exp(log_a)
        a_scr[...] = a
        b_scr[...] = jnp.sqrt(1.0 - a * a) * (g * xc)

    f_starts, f_ends, f_group = chunk_info(i)
    cb = nchunk - 1 - i
    b_starts, b_ends, b_group = chunk_info(cb)
    gates(xf_ref, xfp_ref, xfn_ref, f_starts, f_ends, 0, af_scr, bf_scr)
    gates(xb_ref, xbp_ref, xbn_ref, b_starts, b_ends, 1, ab_scr, bb_scr)

    @pl.when(f_starts)
    def _():
        sf_scr[...] = jnp.broadcast_to(h0f_ref[pl.ds(f_group, 1), :], sf_scr.shape)

    @pl.when(b_ends)
    def _():
        sb_scr[...] = jnp.broadcast_to(h0b_ref[pl.ds(b_group, 1), :], sb_scr.shape)

    ntile = ts // SUBLANES

    def tile_step(k, carry):
        hf, hb = carry
        base_f = pl.multiple_of(k * SUBLANES, SUBLANES)
        base_b = pl.multiple_of((ntile - 1 - k) * SUBLANES, SUBLANES)
        a_f = af_scr[pl.ds(base_f, SUBLANES), :]
        b_f = bf_scr[pl.ds(base_f, SUBLANES), :]
        a_b = ab_scr[pl.ds(base_b, SUBLANES), :]
        b_b = bb_scr[pl.ds(base_b, SUBLANES), :]
        out_f = jnp.zeros((SUBLANES, LRU_WIDTH), F32)
        out_b = jnp.zeros((SUBLANES, LRU_WIDTH), F32)
        for r in range(SUBLANES):
            rb = SUBLANES - 1 - r
            hf = a_f[r:r + 1, :] * hf + b_f[r:r + 1, :]
            hb = a_b[rb:rb + 1, :] * hb + b_b[rb:rb + 1, :]
            out_f = jnp.where(tile_row == r, hf, out_f)
            out_b = jnp.where(tile_row == rb, hb, out_b)
        hf_ref[pl.ds(base_f, SUBLANES), :] = out_f
        hb_ref[pl.ds(base_b, SUBLANES), :] = out_b
        return hf, hb

    hf, hb = lax.fori_loop(0, ntile, tile_step, (sf_scr[0:1, :], sb_scr[0:1, :]))
    sf_scr[...] = jnp.broadcast_to(hf, sf_scr.shape)
    sb_scr[...] = jnp.broadcast_to(hb, sb_scr.shape)


def _lru(z1, conv_w, conv_b, wa_hi, wa_lo, wx_hi, wx_lo, ba, bx, lam, h0f, h0b, *, nctx, dec_seq):
    n = z1.shape[0]
    ts = TOKEN_TILE
    nchunk = n // ts
    xcol = COL_XR // LRU_WIDTH
    tiles = ts // SUBLANES
    last_tile = n // SUBLANES - 1

    def cur(rev):
        return lambda i: ((nchunk - 1 - i) if rev else i, xcol)

    def prev(rev):
        return lambda i: (jnp.maximum(((nchunk - 1 - i) if rev else i) * tiles - 1, 0), xcol)

    def nxt(rev):
        return lambda i: (jnp.minimum((((nchunk - 1 - i) if rev else i) + 1) * tiles, last_tile), xcol)

    x_specs = []
    for rev in (False, True):
        x_specs += [pl.BlockSpec((ts, LRU_WIDTH), cur(rev)),
                    pl.BlockSpec((SUBLANES, LRU_WIDTH), prev(rev)),
                    pl.BlockSpec((SUBLANES, LRU_WIDTH), nxt(rev))]
    full = lambda shape: pl.BlockSpec(shape, lambda i: (0,) * len(shape))
    w_spec = full((2, LRU_WIDTH, LRU_WIDTH))
    v_spec = full((2, 1, LRU_WIDTH))
    out_shape = jax.ShapeDtypeStruct((n, LRU_WIDTH), F32)
    scr = pltpu.VMEM((ts, LRU_WIDTH), F32)
    state = pltpu.VMEM((SUBLANES, LRU_WIDTH), F32)
    return pl.pallas_call(
        functools.partial(_lru_kernel, nctx_blk=nctx // ts, seq_blk=dec_seq // ts),
        grid=(nchunk,),
        in_specs=x_specs + [full((4, LRU_WIDTH)), full((1, LRU_WIDTH)),
                            w_spec, w_spec, w_spec, w_spec, v_spec, v_spec, v_spec,
                            full((SUBLANES, LRU_WIDTH)), full((SUBLANES, LRU_WIDTH))],
        out_specs=[pl.BlockSpec((ts, LRU_WIDTH), lambda i: (i, 0)),
                   pl.BlockSpec((ts, LRU_WIDTH), lambda i: (nchunk - 1 - i, 0))],
        out_shape=[out_shape, out_shape],
        scratch_shapes=[scr, scr, scr, scr, state, state],
        compiler_params=_cparams(("arbitrary",)),
        name="lru",
    )(z1, z1, z1, z1, z1, z1, conv_w, conv_b, wa_hi, wa_lo, wx_hi, wx_lo, ba, bx, lam, h0f, h0b)


def _merge_kernel(x_ref, mod_ref, oa_ref, ob_ref, hf_ref, hb_ref, z2_ref,
                  woa_ref, wob_ref, woc_ref, wout_ref, o_ref):
    d = x_ref.shape[1]
    oc = (hf_ref[...] + hb_ref[...]) * _gelu(z2_ref[:, 0:LRU_WIDTH])
    ga = z2_ref[:, LRU_WIDTH:LRU_WIDTH + d]
    gb = z2_ref[:, LRU_WIDTH + d:LRU_WIDTH + 2 * d]
    gc = z2_ref[:, LRU_WIDTH + 2 * d:LRU_WIDTH + 3 * d]
    merged = (_sigmoid(ga) * _dot(oa_ref[...], woa_ref[...])
              + _sigmoid(gb) * _dot(ob_ref[...], wob_ref[...])
              + _sigmoid(gc) * _dot(oc.astype(BF16), woc_ref[...]))
    out = _dot(merged.astype(BF16), wout_ref[...])
    gate = mod_ref[0][:, 2 * d:3 * d]
    o_ref[...] = x_ref[...] + gate * out


def _merge(x, mod3, oa, ob, hf, hb, z2, wo_a, wo_b, wo_c, w_out, *, nctx, dec_seq):
    n, d = x.shape
    tm = TOKEN_TILE
    row = lambda w: pl.BlockSpec((tm, w), lambda i: (i, 0))
    full = lambda a: pl.BlockSpec(a.shape, lambda i: (0, 0))
    return pl.pallas_call(
        _merge_kernel,
        grid=(n // tm,),
        in_specs=[row(d),
                  pl.BlockSpec((1, 1, mod3.shape[2]),
                               lambda i: (_group_of_block(i, tm, nctx, dec_seq), 0, 0)),
                  row(Q_WIDTH), row(Q_WIDTH), row(LRU_WIDTH), row(LRU_WIDTH), row(Z2_WIDTH),
                  full(wo_a), full(wo_b), full(wo_c), full(w_out)],
        out_specs=row(d),
        out_shape=jax.ShapeDtypeStruct((n, d), F32),
        compiler_params=_cparams(("parallel",)),
        name="merge",
    )(x, mod3, oa, ob, hf, hb, z2, wo_a, wo_b, wo_c, w_out)


def _top_rows(s, k, extras=()):
    nrow = s.shape[0]
    rows = lax.broadcasted_iota(jnp.int32, s.shape, 0).astype(F32)
    vals, idxs, picked = [], [], [[] for _ in extras]
    for _ in range(k):
        m = jnp.max(s, axis=0, keepdims=True)
        pos = jnp.min(jnp.where(s == m, rows, float(nrow)), axis=0, keepdims=True)
        sel = rows == pos
        vals.append(m)
        idxs.append(pos)
        for dst, ex in zip(picked, extras):
            dst.append(jnp.max(jnp.where(sel, ex, -1.0), axis=0, keepdims=True))
        s = jnp.where(sel, -jnp.inf, s)
    cat = lambda xs: jnp.concatenate(xs, axis=0)
    return cat(vals), cat(idxs), [cat(p) for p in picked]


def _peer_topk_kernel(q_ref, khi_ref, klo_ref, g_ref, i1_ref, i2_ref):
    top_s, top_i = [], []
    for p in range(2):
        q_hi, q_lo = _split(q_ref[:, p * PEER_DKEY:(p + 1) * PEER_DKEY])
        k_hi, k_lo = khi_ref[0, p], klo_ref[0, p]
        s = _dot_nt(k_hi, q_hi) + _dot_nt(k_lo, q_hi) + _dot_nt(k_hi, q_lo)
        v, ix, _ = _top_rows(s, PEER_TOPK)
        top_s.append(v)
        top_i.append(ix)
    tb = top_s[0].shape[1]
    bshape = (PEER_TOPK, tb)
    cand_s = jnp.concatenate([jnp.broadcast_to(top_s[0][a:a + 1], bshape) + top_s[1]
                              for a in range(PEER_TOPK)], axis=0)
    cand_1 = jnp.concatenate([jnp.broadcast_to(top_i[0][a:a + 1], bshape)
                              for a in range(PEER_TOPK)], axis=0)
    cand_2 = jnp.concatenate([top_i[1]] * PEER_TOPK, axis=0)
    best, _, (e1, e2) = _top_rows(cand_s, PEER_TOPK, (cand_1, cand_2))
    ex = jnp.exp(best - best[0:1])
    g_ref[0] = ex / jnp.sum(ex, axis=0, keepdims=True)
    i1_ref[0] = e1
    i2_ref[0] = e2


def _peer_topk(q, keys_hi, keys_lo):
    n = q.shape[0]
    tb = PEER_TB
    spec = pl.BlockSpec((1, PEER_TOPK, tb), lambda i, h: (h, 0, i))
    shape = jax.ShapeDtypeStruct((PEER_HEADS, PEER_TOPK, n), F32)
    k_spec = pl.BlockSpec((1, 2, PEER_NKEYS, PEER_DKEY), lambda i, h: (h, 0, 0, 0))
    return pl.pallas_call(
        _peer_topk_kernel,
        grid=(n // tb, PEER_HEADS),
        in_specs=[pl.BlockSpec((tb, 2 * PEER_DKEY), lambda i, h: (i, h)), k_spec, k_spec],
        out_specs=[spec, spec, spec],
        out_shape=[shape, shape, shape],
        compiler_params=_cparams(("parallel", "parallel")),
        name="peer_topk",
    )(q, keys_hi, keys_lo)


def _peer_expert_kernel(x_ref, mod_ref, gn_ref, g_ref, i1_ref, i2_ref, ut_ref, v_ref, o_ref,
                        h_scr, w_scr, p_scr, acc_scr):
    c = pl.program_id(1)
    tb, d = x_ref.shape
    ec = ut_ref.shape[1]
    sub = ec // PEER_NKEYS
    npair = PEER_HEADS * PEER_TOPK

    @pl.when(c == 0)
    def _():
        x = x_ref[...]
        y = x * lax.rsqrt(jnp.mean(x * x, axis=-1, keepdims=True) + EPS) * gn_ref[...]
        m = mod_ref[0]
        h_scr[...] = (y * (1.0 + m[:, 4 * d:5 * d]) + m[:, 3 * d:4 * d]).astype(BF16)
        acc_scr[...] = jnp.zeros(acc_scr.shape, F32)
        key_row = lax.broadcasted_iota(jnp.int32, (PEER_NKEYS, npair), 0).astype(F32)

        def token(t, carry):
            gate = g_ref[pl.ds(t, 1), :]
            ga = jnp.where(key_row == i1_ref[pl.ds(t, 1), :], gate, 0.0).astype(BF16)
            ob = jnp.where(key_row == i2_ref[pl.ds(t, 1), :], 1.0, 0.0).astype(BF16)
            w_scr[pl.ds(pl.multiple_of(t * PEER_NKEYS, PEER_NKEYS), PEER_NKEYS), :] = _dot_nt(ga, ob)
            return carry

        lax.fori_loop(0, tb, token, 0)

    h1 = _dot(h_scr[...], ut_ref[...])
    for s in range(sub):
        w = w_scr[pl.ds(c * sub + s, tb, stride=PEER_NKEYS), :]
        p_scr[:, s * PEER_NKEYS:(s + 1) * PEER_NKEYS] = (
            w * _gelu(h1[:, s * PEER_NKEYS:(s + 1) * PEER_NKEYS])).astype(BF16)
    acc_scr[...] += _dot(p_scr[...], v_ref[...])

    @pl.when(c == pl.num_programs(1) - 1)
    def _():
        o_ref[...] = x_ref[...] + mod_ref[0][:, 5 * d:6 * d] * acc_scr[...]


def _peer_expert(x, mod3, gain, g, i1, i2, ut, v, *, nctx, dec_seq):
    n, d = x.shape
    tb, ec = PEER_TB, PEER_EC
    nexp = v.shape[0]
    npair = PEER_HEADS * PEER_TOPK
    row = lambda w: pl.BlockSpec((tb, w), lambda i, c: (i, 0))
    return pl.pallas_call(
        _peer_expert_kernel,
        grid=(n // tb, nexp // ec),
        in_specs=[row(d),
                  pl.BlockSpec((1, 1, mod3.shape[2]),
                               lambda i, c: (_group_of_block(i, tb, nctx, dec_seq), 0, 0)),
                  pl.BlockSpec((1, d), lambda i, c: (0, 0)),
                  row(npair), row(npair), row(npair),
                  pl.BlockSpec((d, ec), lambda i, c: (0, c)),
                  pl.BlockSpec((ec, d), lambda i, c: (c, 0))],
        out_specs=row(d),
        out_shape=jax.ShapeDtypeStruct((n, d), F32),
        scratch_shapes=[pltpu.VMEM((tb, d), BF16),
                        pltpu.VMEM((tb * PEER_NKEYS, PEER_NKEYS), F32),
                        pltpu.VMEM((tb, ec), BF16),
                        pltpu.VMEM((tb, d), F32)],
        compiler_params=_cparams(("parallel", "arbitrary")),
        name="peer_expert",
    )(x, mod3, gain.reshape(1, d), g, i1, i2, ut, v)


def _final_norm_kernel(x_ref, g_ref, o_ref):
    x = x_ref[...]
    o_ref[...] = x * lax.rsqrt(jnp.mean(x * x, axis=-1, keepdims=True) + EPS) * g_ref[...]


def _final_norm(x, gain):
    n, d = x.shape
    tm = 512
    return pl.pallas_call(
        _final_norm_kernel,
        grid=(n // tm,),
        in_specs=[pl.BlockSpec((tm, d), lambda i: (i, 0)), pl.BlockSpec((1, d), lambda i: (0, 0))],
        out_specs=pl.BlockSpec((tm, d), lambda i: (i, 0)),
        out_shape=jax.ShapeDtypeStruct((n, d), F32),
        compiler_params=_cparams(("parallel",)),
        name="final_norm",
    )(x, gain.reshape(1, d))


def _rope_tables(dec_seq, tile):
    t = jnp.arange(dec_seq)
    n_freq = HEAD_DIM // 4
    inv = ROPE_BASE ** (-jnp.arange(n_freq, dtype=F32) / n_freq)
    ang = jnp.concatenate([(t // GRID_W).astype(F32)[:, None] * inv,
                           (t % GRID_W).astype(F32)[:, None] * inv], axis=-1)
    cos, sin = jnp.cos(ang), jnp.sin(ang)
    reps = LANES // HEAD_DIM
    cos_t = jnp.tile(jnp.concatenate([cos, cos], axis=-1), (1, reps))
    sin_t = jnp.tile(jnp.concatenate([-sin, sin], axis=-1), (1, reps))
    cos_t = jnp.concatenate([jnp.ones((tile, LANES), F32), cos_t], axis=0)
    sin_t = jnp.concatenate([jnp.zeros((tile, LANES), F32), sin_t], axis=0)
    return cos_t, sin_t


def _dup_cache(cache):
    c = jnp.transpose(cache, (0, 2, 1, 3))
    return jnp.concatenate([c, c], axis=-1).astype(BF16)


def _dense_blockdiag(w):
    dirs, nb, bw, _ = w.shape
    eye = jnp.eye(nb, dtype=w.dtype)
    return jnp.einsum('dncf,nm->dncmf', w, eye).reshape(dirs, nb * bw, nb * bw)


def _pad_rows(a, rows):
    return jnp.concatenate([a, jnp.zeros((rows - a.shape[0],) + a.shape[1:], a.dtype)], axis=0)


def kernel(x_prompt, x_sample, c, cache_wa_k, cache_wa_v, cache_ax_k, cache_ax_v, state_lru_fwd,
           state_lru_bwd, c_ctx, w_ada, b_ada, g_norm1, w_in, wa_sink, ax_q_gain, ax_k_gain, conv_w,
           conv_b, lru_wa, lru_ba, lru_wx, lru_bx, lru_lambda, wo_a, wo_b, wo_c, w_out, g_norm2,
           peer_wq, peer_keys, peer_u, peer_v, g_final):
    batch, seq, d = x_prompt.shape
    dec_batch, dec_seq, _ = x_sample.shape
    depth = w_in.shape[0]
    nctx = batch * seq
    nlat = dec_batch * dec_seq
    assert seq == TOKEN_TILE and dec_seq % TOKEN_TILE == 0 and 1 + dec_batch <= SUBLANES
    assert w_in.shape[2] == Z1_WIDTH + Z2_WIDTH and d == 1024
    sizes = dict(nctx=nctx, dec_seq=dec_seq)

    x = jnp.concatenate([x_prompt.reshape(nctx, d), x_sample.reshape(nlat, d)], axis=0)
    cvec = _pad_rows(jnp.concatenate([c_ctx[None, :], c], axis=0), SUBLANES)
    mods = _ada(cvec, w_ada, b_ada)

    cos_tab, sin_tab = _rope_tables(dec_seq, TOKEN_TILE)
    head_mean = jnp.kron(jnp.eye(LANES // HEAD_DIM, dtype=F32),
                         jnp.full((HEAD_DIM, HEAD_DIM), 1.0 / HEAD_DIM, F32)).astype(BF16)
    reps = LANES // HEAD_DIM

    new_wa_k, new_wa_v, new_ax_k, new_ax_v, new_hf, new_hb = [], [], [], [], [], []
    for l in range(depth):
        mod3 = mods[l].reshape(SUBLANES, 1, 6 * d)
        w_in_bf = w_in[l].astype(BF16)
        z1 = _modmm(x, mod3, g_norm1[l], w_in_bf[:, :Z1_WIDTH], None, shift_col=0, tn=512, **sizes)
        z2 = _modmm(x, mod3, g_norm1[l], w_in_bf[:, Z1_WIDTH:], None, shift_col=0, tn=512, **sizes)

        gq = jnp.tile(ax_q_gain[l], reps).reshape(1, LANES)
        gk = jnp.tile(ax_k_gain[l], reps).reshape(1, LANES)
        qsa, kka, vva, qsb, kkb, vvb, kbn = _prep(z1, cos_tab, sin_tab, gq, gk, head_mean, **sizes)

        oa_ctx, ob_ctx = _att_ctx(wa_sink[l], qsa, kka, vva, qsb, kkb, vvb, nseq=batch, seq=seq)
        oa_lat = _att_win(wa_sink[l], qsa, kka, vva, _dup_cache(cache_wa_k[:, l]),
                          _dup_cache(cache_wa_v[:, l]), nctx=nctx, dec_batch=dec_batch, dec_seq=dec_seq)

        def with_cache(cache, cur):
            lat = cur[:, nctx:].reshape(KV_HEADS, dec_batch, dec_seq, LANES)
            return jnp.concatenate([_dup_cache(cache), jnp.transpose(lat, (1, 0, 2, 3))], axis=2)

        ob_lat = _att_dense(qsb, with_cache(cache_ax_k[:, l], kkb), with_cache(cache_ax_v[:, l], vvb),
                            nctx=nctx, dec_batch=dec_batch, dec_seq=dec_seq)
        oa = jnp.concatenate([oa_ctx, oa_lat], axis=0)
        ob = jnp.concatenate([ob_ctx, ob_lat], axis=0)

        wa_hi, wa_lo = _split(_dense_blockdiag(lru_wa[l]))
        wx_hi, wx_lo = _split(_dense_blockdiag(lru_wx[l]))
        vec = lambda a: a.reshape(2, 1, LRU_WIDTH)
        h0f = _pad_rows(jnp.concatenate([jnp.zeros((1, LRU_WIDTH), F32), state_lru_fwd[:, l]], axis=0), SUBLANES)
        h0b = _pad_rows(jnp.concatenate([jnp.zeros((1, LRU_WIDTH), F32), state_lru_bwd[:, l]], axis=0), SUBLANES)
        hf, hb = _lru(z1, conv_w[l], conv_b[l].reshape(1, LRU_WIDTH), wa_hi, wa_lo, wx_hi, wx_lo,
                      vec(lru_ba[l]), vec(lru_bx[l]), vec(lru_lambda[l]), h0f, h0b, **sizes)

        x = _merge(x, mod3, oa, ob, hf, hb, z2, wo_a[l].astype(BF16), wo_b[l].astype(BF16),
                   wo_c[l].astype(BF16), w_out[l].astype(BF16), **sizes)

        wq_hi, wq_lo = _split(peer_wq[l])
        q = _modmm(x, mod3, g_norm2[l], wq_hi, wq_lo, shift_col=3 * d, tn=512, **sizes)
        keys_hi, keys_lo = _split(peer_keys[l])
        g, i1, i2 = _peer_topk(q, keys_hi, keys_lo)
        pairs = lambda a: jnp.transpose(a, (2, 0, 1)).reshape(nctx + nlat, PEER_HEADS * PEER_TOPK)
        x = _peer_expert(x, mod3, g_norm2[l], pairs(g), pairs(i1), pairs(i2),
                         peer_u[l].T.astype(BF16), peer_v[l].astype(BF16), **sizes)

        ctx4 = lambda a: a[:nctx].reshape(batch, seq, KV_HEADS, HEAD_DIM)
        new_wa_k.append(ctx4(z1[:, COL_KA:COL_KA + KV_WIDTH]))
        new_wa_v.append(ctx4(z1[:, COL_VA:COL_VA + KV_WIDTH]))
        new_ax_k.append(ctx4(kbn))
        new_ax_v.append(ctx4(z1[:, COL_VB:COL_VB + KV_WIDTH]))
        new_hf.append(hf[:nctx].reshape(batch, seq, LRU_WIDTH)[:, -1])
        new_hb.append(hb[:nctx].reshape(batch, seq, LRU_WIDTH)[:, 0])

    y = _final_norm(x, g_final)
    stack = lambda xs: jnp.stack(xs, axis=1)
    return (y[:nctx].reshape(batch, seq, d), y[nctx:].reshape(dec_batch, dec_seq, d),
            stack(new_wa_k), stack(new_wa_v), stack(new_ax_k), stack(new_ax_v),
            stack(new_hf), stack(new_hb))
```

```python
import functools

import jax
import jax.numpy as jnp
from jax import lax
from jax.experimental import pallas as pl
from jax.experimental.pallas import tpu as pltpu

F32 = jnp.float32
BF16 = jnp.bfloat16

HEAD_DIM = 64
KV_HEADS = 2
Q_GROUP = 4
Q_WIDTH = KV_HEADS * Q_GROUP * HEAD_DIM
KV_WIDTH = KV_HEADS * HEAD_DIM
WINDOW = 128
GRID_W = 64
LRU_WIDTH = 512
LRU_BLOCKS = 8
LRU_C = 8.0
PEER_HEADS = 8
PEER_NKEYS = 128
PEER_DKEY = 128
PEER_TOPK = 16
ROPE_BASE = 10000.0
EPS = 1e-6
NEG = -1e30
LANES = 128
SUBLANES = 8
VMEM_LIMIT = 56 * 1024 * 1024

Z1_WIDTH = 2048
Z2_WIDTH = 3584
COL_KA, COL_VA, COL_QB, COL_KB, COL_VB, COL_XR = 512, 640, 768, 1280, 1408, 1536

TOKEN_TILE = 256
ATT_A_TQ = 128
ATT_B_TQ = 256
ATT_B_TK = (1536, 1024, 512)
PEER_TB = 256
PEER_EC = 2048
PEER_TOKEN_UNROLL = 16
PEER_W_PITCH = PEER_NKEYS + SUBLANES


def _cparams(sem):
    return pltpu.CompilerParams(dimension_semantics=sem, vmem_limit_bytes=VMEM_LIMIT)


def _split(x):
    hi = x.astype(BF16)
    lo = (x - hi.astype(F32)).astype(BF16)
    return hi, lo


def _dot(a, b):
    return jnp.dot(a, b, preferred_element_type=F32)


def _dot_nt(a, b):
    return lax.dot_general(a, b, (((1,), (1,)), ((), ())), preferred_element_type=F32)


def _sigmoid(x):
    return 1.0 / (1.0 + jnp.exp(-x))


def _gelu(x):
    return 0.5 * x * (1.0 + jnp.tanh(0.7978845608028654 * (x + 0.044715 * (x * x * x))))


def _group_of_block(i, tile, nctx, dec_seq):
    row = i * tile
    return jnp.where(row < nctx, 0, 1 + (row - nctx) // dec_seq)


def _ada_kernel(c_ref, w_ref, b_ref, o_ref):
    c = c_ref[...]
    s = c * _sigmoid(c)
    s_hi, s_lo = _split(s)
    w_hi, w_lo = _split(w_ref[0])
    o_ref[0] = _dot(s_hi, w_hi) + _dot(s_lo, w_hi) + _dot(s_hi, w_lo) + b_ref[0]


def _ada(cvec, w_ada, b_ada):
    depth, d, e = w_ada.shape
    tn = 1536
    return pl.pallas_call(
        _ada_kernel,
        grid=(depth, e // tn),
        in_specs=[
            pl.BlockSpec((SUBLANES, d), lambda l, j: (0, 0)),
            pl.BlockSpec((1, d, tn), lambda l, j: (l, 0, j)),
            pl.BlockSpec((1, 1, tn), lambda l, j: (l, 0, j)),
        ],
        out_specs=pl.BlockSpec((1, SUBLANES, tn), lambda l, j: (l, 0, j)),
        out_shape=jax.ShapeDtypeStruct((depth, SUBLANES, e), F32),
        compiler_params=_cparams(("parallel", "parallel")),
        name="ada",
    )(cvec, w_ada, b_ada.reshape(depth, 1, e))


def _modmm_kernel(x_ref, mod_ref, g_ref, w_ref, *rest, shift_col, three_pass):
    d = x_ref.shape[1]
    if three_pass:
        wlo_ref, o_ref, h_scr, hlo_scr = rest
    else:
        o_ref, h_scr = rest

    @pl.when(pl.program_id(1) == 0)
    def _():
        x = x_ref[...]
        y = x * lax.rsqrt(jnp.mean(x * x, axis=-1, keepdims=True) + EPS) * g_ref[...]
        m = mod_ref[0]
        h = y * (1.0 + m[:, shift_col + d:shift_col + 2 * d]) + m[:, shift_col:shift_col + d]
        hi = h.astype(BF16)
        h_scr[...] = hi
        if three_pass:
            hlo_scr[...] = (h - hi.astype(F32)).astype(BF16)

    acc = _dot(h_scr[...], w_ref[...])
    if three_pass:
        acc = acc + _dot(hlo_scr[...], w_ref[...]) + _dot(h_scr[...], wlo_ref[...])
    o_ref[...] = acc


def _modmm(x, mod3, gain, w, w_lo, *, shift_col, tn, nctx, dec_seq):
    n, d = x.shape
    width = w.shape[1]
    tm = 512
    three_pass = w_lo is not None
    grp = lambda i, j: (_group_of_block(i, tm, nctx, dec_seq), 0, 0)
    in_specs = [
        pl.BlockSpec((tm, d), lambda i, j: (i, 0)),
        pl.BlockSpec((1, 1, mod3.shape[2]), grp),
        pl.BlockSpec((1, d), lambda i, j: (0, 0)),
        pl.BlockSpec((d, tn), lambda i, j: (0, j)),
    ]
    args = [x, mod3, gain.reshape(1, d), w]
    scratch = [pltpu.VMEM((tm, d), BF16)]
    if three_pass:
        in_specs.append(pl.BlockSpec((d, tn), lambda i, j: (0, j)))
        args.append(w_lo)
        scratch.append(pltpu.VMEM((tm, d), BF16))
    return pl.pallas_call(
        functools.partial(_modmm_kernel, shift_col=shift_col, three_pass=three_pass),
        grid=(n // tm, width // tn),
        in_specs=in_specs,
        out_specs=pl.BlockSpec((tm, tn), lambda i, j: (i, j)),
        out_shape=jax.ShapeDtypeStruct((n, width), F32),
        scratch_shapes=scratch,
        compiler_params=_cparams(("parallel", "arbitrary")),
        name="modmm",
    )(*args)


def _prep_kernel(z_ref, cos_ref, sin_ref, gq_ref, gk_ref, m_ref,
                 qsa_ref, kka_ref, vva_ref, qsb_ref, kkb_ref, vvb_ref, kbn_ref, qtb_ref):
    tm = z_ref.shape[0]
    lane = lax.broadcasted_iota(jnp.int32, (tm, LANES), 1)
    low_head = lane < HEAD_DIM
    first_half = (lane & (HEAD_DIM // 2)) == 0
    cos = cos_ref[...]
    sin = sin_ref[...]
    mmat = m_ref[...]
    scale = HEAD_DIM ** -0.5

    def rope(x):
        back = pltpu.roll(x, HEAD_DIM // 2, 1)
        fwd = pltpu.roll(x, LANES - HEAD_DIM // 2, 1)
        return x * cos + jnp.where(first_half, fwd, back) * sin

    def headnorm(x, gain):
        sq_hi, sq_lo = _split(x * x)
        ms = _dot(sq_hi, mmat) + _dot(sq_lo, mmat)
        return x * lax.rsqrt(ms + EPS) * gain

    def store_q(ref, c, q, t_ref=None):
        zero = jnp.zeros_like(q)
        j, g0 = c // 2, 2 * (c % 2)
        for g, qm in ((g0, jnp.where(low_head, q, zero)), (g0 + 1, jnp.where(low_head, zero, q))):
            ref[j, g] = qm.astype(BF16)
            if t_ref is not None:
                t_ref[j, g] = qm.T.astype(BF16)

    def store_dup(ref, x):
        swapped = pltpu.roll(x, HEAD_DIM, 1)
        ref[0] = jnp.where(low_head, x, swapped).astype(BF16)
        ref[1] = jnp.where(low_head, swapped, x).astype(BF16)

    for c in range(Q_WIDTH // LANES):
        store_q(qsa_ref, c, rope(z_ref[:, c * LANES:(c + 1) * LANES]) * scale)
        qb = headnorm(z_ref[:, COL_QB + c * LANES:COL_QB + (c + 1) * LANES], gq_ref[...])
        store_q(qsb_ref, c, rope(qb) * scale, qtb_ref)
    store_dup(kka_ref, rope(z_ref[:, COL_KA:COL_KA + KV_WIDTH]))
    store_dup(vva_ref, z_ref[:, COL_VA:COL_VA + KV_WIDTH])
    kb = rope(headnorm(z_ref[:, COL_KB:COL_KB + KV_WIDTH], gk_ref[...]))
    kbn_ref[...] = kb
    store_dup(kkb_ref, kb)
    store_dup(vvb_ref, z_ref[:, COL_VB:COL_VB + KV_WIDTH])


def _prep(z1, cos_tab, sin_tab, gq, gk, mmat, *, nctx, dec_seq):
    n = z1.shape[0]
    tm = TOKEN_TILE
    nctx_blk = nctx // tm
    seq_blk = dec_seq // tm

    def tab_idx(i):
        return (jnp.where(i < nctx_blk, 0, 1 + (i - nctx_blk) % seq_blk), 0)

    qs_spec = pl.BlockSpec((KV_HEADS, Q_GROUP, tm, LANES), lambda i: (0, 0, i, 0))
    kv_spec = pl.BlockSpec((KV_HEADS, tm, LANES), lambda i: (0, i, 0))
    qs_shape = jax.ShapeDtypeStruct((KV_HEADS, Q_GROUP, n, LANES), BF16)
    kv_shape = jax.ShapeDtypeStruct((KV_HEADS, n, LANES), BF16)
    return pl.pallas_call(
        _prep_kernel,
        grid=(n // tm,),
        in_specs=[
            pl.BlockSpec((tm, Z1_WIDTH), lambda i: (i, 0)),
            pl.BlockSpec((tm, LANES), tab_idx),
            pl.BlockSpec((tm, LANES), tab_idx),
            pl.BlockSpec((1, LANES), lambda i: (0, 0)),
            pl.BlockSpec((1, LANES), lambda i: (0, 0)),
            pl.BlockSpec((LANES, LANES), lambda i: (0, 0)),
        ],
        out_specs=[qs_spec, kv_spec, kv_spec, qs_spec, kv_spec, kv_spec,
                   pl.BlockSpec((tm, LANES), lambda i: (i, 0)),
                   pl.BlockSpec((KV_HEADS, Q_GROUP, LANES, tm), lambda i: (0, 0, 0, i))],
        out_shape=[qs_shape, kv_shape, kv_shape, qs_shape, kv_shape, kv_shape,
                   jax.ShapeDtypeStruct((n, LANES), F32),
                   jax.ShapeDtypeStruct((KV_HEADS, Q_GROUP, LANES, n), BF16)],
        compiler_params=_cparams(("parallel",)),
        name="prep",
    )(z1, cos_tab, sin_tab, gq, gk, mmat)


def _merge_heads(o, tq, c):
    lane = lax.broadcasted_iota(jnp.int32, (tq, LANES), 1)
    g = 2 * c
    return jnp.where(lane < HEAD_DIM, o[g * tq:(g + 1) * tq], o[(g + 1) * tq:(g + 2) * tq])


def _sink_column(sink_ref, j, tq):
    return jnp.concatenate(
        [jnp.full((tq, 1), sink_ref[j * Q_GROUP + g], F32) for g in range(Q_GROUP)], axis=0)


def _att_ctx_kernel(sink_ref, qsa_ref, kka_ref, vva_ref, qsb_ref, kkb_ref, vvb_ref, oa_ref, ob_ref):
    tq = qsa_ref.shape[2]
    for q_ref, k_ref, v_ref, o_ref, has_sink in (
            (qsa_ref, kka_ref, vva_ref, oa_ref, True), (qsb_ref, kkb_ref, vvb_ref, ob_ref, False)):
        for j in range(KV_HEADS):
            qs = q_ref[j].reshape(Q_GROUP * tq, LANES)
            s = _dot_nt(qs, k_ref[j])
            m = jnp.max(s, axis=-1, keepdims=True)
            if has_sink:
                sk = _sink_column(sink_ref, j, tq)
                m = jnp.maximum(m, sk)
            e = jnp.exp(s - m)
            den = jnp.sum(e, axis=-1, keepdims=True)
            if has_sink:
                den = den + jnp.exp(sk - m)
            o = _dot(e.astype(BF16), v_ref[j]) / den
            for c in range(2):
                col = (2 * j + c) * LANES
                o_ref[:, col:col + LANES] = _merge_heads(o, tq, c).astype(BF16)


def _att_ctx(sink, qsa, kka, vva, qsb, kkb, vvb, *, nseq, seq):
    qs_spec = pl.BlockSpec((KV_HEADS, Q_GROUP, seq, LANES), lambda i: (0, 0, i, 0))
    kv_spec = pl.BlockSpec((KV_HEADS, seq, LANES), lambda i: (0, i, 0))
    o_spec = pl.BlockSpec((seq, Q_WIDTH), lambda i: (i, 0))
    o_shape = jax.ShapeDtypeStruct((nseq * seq, Q_WIDTH), BF16)
    return pl.pallas_call(
        _att_ctx_kernel,
        grid=(nseq,),
        in_specs=[pl.BlockSpec(memory_space=pltpu.SMEM),
                  qs_spec, kv_spec, kv_spec, qs_spec, kv_spec, kv_spec],
        out_specs=[o_spec, o_spec],
        out_shape=[o_shape, o_shape],
        compiler_params=_cparams(("parallel",)),
        name="att_ctx",
    )(sink, qsa, kka, vva, qsb, kkb, vvb)


def _att_win_kernel(sink_ref, qs_ref, kp_ref, kc_ref, kn_ref, vp_ref, vc_ref, vn_ref,
                    kx_ref, vx_ref, o_ref, *, nblk):
    i = pl.program_id(1)
    tq = qs_ref.shape[2]
    rows = Q_GROUP * tq
    a_idx = lax.broadcasted_iota(jnp.int32, (rows, tq), 0) & (tq - 1)
    j_idx = lax.broadcasted_iota(jnp.int32, (rows, tq), 1)
    valid_prev = (j_idx >= a_idx) & (i > 0)
    valid_next = (j_idx <= a_idx) & (i < nblk - 1)
    for j in range(KV_HEADS):
        qs = qs_ref[j].reshape(rows, LANES)
        s_x = _dot_nt(qs, kx_ref[j])
        s_p = jnp.where(valid_prev, _dot_nt(qs, kp_ref[j]), NEG)
        s_c = _dot_nt(qs, kc_ref[j])
        s_n = jnp.where(valid_next, _dot_nt(qs, kn_ref[j]), NEG)
        sk = _sink_column(sink_ref, j, tq)
        m = jnp.maximum(jnp.max(s_x, axis=-1, keepdims=True), sk)
        for s in (s_p, s_c, s_n):
            m = jnp.maximum(m, jnp.max(s, axis=-1, keepdims=True))
        den = jnp.exp(sk - m)
        o = jnp.zeros((rows, LANES), F32)
        for s, v_ref in ((s_x, vx_ref), (s_p, vp_ref), (s_c, vc_ref), (s_n, vn_ref)):
            e = jnp.exp(s - m)
            den = den + jnp.sum(e, axis=-1, keepdims=True)
            o = o + _dot(e.astype(BF16), v_ref[j])
        o = o / den
        for c in range(2):
            col = (2 * j + c) * LANES
            o_ref[:, col:col + LANES] = _merge_heads(o, tq, c).astype(BF16)


def _att_win(sink, qs, kk, vv, kx, vx, *, nctx, dec_batch, dec_seq):
    tq = ATT_A_TQ
    assert tq == WINDOW
    nblk = dec_seq // tq
    base = nctx // tq
    past = kx.shape[2]

    def q_idx(b, i):
        return (0, 0, base + b * nblk + i, 0)

    def kv_idx(delta):
        return lambda b, i: (0, base + b * nblk + jnp.clip(i + delta, 0, nblk - 1), 0)

    kv_specs = [pl.BlockSpec((KV_HEADS, tq, LANES), kv_idx(d)) for d in (-1, 0, 1)]
    x_spec = pl.BlockSpec((None, KV_HEADS, past, LANES), lambda b, i: (b, 0, 0, 0))
    return pl.pallas_call(
        functools.partial(_att_win_kernel, nblk=nblk),
        grid=(dec_batch, nblk),
        in_specs=[pl.BlockSpec(memory_space=pltpu.SMEM),
                  pl.BlockSpec((KV_HEADS, Q_GROUP, tq, LANES), q_idx)] + kv_specs + kv_specs
                 + [x_spec, x_spec],
        out_specs=pl.BlockSpec((tq, Q_WIDTH), lambda b, i: (b * nblk + i, 0)),
        out_shape=jax.ShapeDtypeStruct((dec_batch * dec_seq, Q_WIDTH), BF16),
        compiler_params=_cparams(("parallel", "parallel")),
        name="att_win",
    )(sink, qs, kk, kk, kk, vv, vv, vv, kx, vx)


def _att_dense_kernel(qt_ref, k_ref, vt_ref, o_ref, m_scr, l_scr, acc_scr):
    kc = pl.program_id(2)
    tq = qt_ref.shape[3]

    @pl.when(kc == 0)
    def _():
        m_scr[...] = jnp.full(m_scr.shape, -jnp.inf, F32)
        l_scr[...] = jnp.zeros(l_scr.shape, F32)
        acc_scr[...] = jnp.zeros(acc_scr.shape, F32)

    for j in range(KV_HEADS):
        for g in range(Q_GROUP):
            h = j * Q_GROUP + g
            st = _dot(k_ref[j], qt_ref[j, g])
            m_prev = m_scr[h]
            m_new = jnp.maximum(m_prev, jnp.max(st, axis=0, keepdims=True))
            alpha = jnp.exp(m_prev - m_new)
            p = jnp.exp(st - m_new)
            l_scr[h] = alpha * l_scr[h] + jnp.sum(p, axis=0, keepdims=True)
            acc_scr[h] = alpha * acc_scr[h] + _dot(vt_ref[j], p.astype(BF16))
            m_scr[h] = m_new

    @pl.when(kc == pl.num_programs(2) - 1)
    def _():
        lane = lax.broadcasted_iota(jnp.int32, (tq, LANES), 1)
        for c in range(KV_HEADS * Q_GROUP // 2):
            even = (acc_scr[2 * c] / l_scr[2 * c]).T
            odd = (acc_scr[2 * c + 1] / l_scr[2 * c + 1]).T
            o_ref[:, c * LANES:(c + 1) * LANES] = jnp.where(lane < HEAD_DIM, even, odd).astype(BF16)


def _att_dense(qt, k_all, vt_all, *, nctx, dec_batch, dec_seq):
    tq = ATT_B_TQ
    nblk = dec_seq // tq
    base = nctx // tq
    nkeys = k_all.shape[2]
    tk = next(t for t in ATT_B_TK if nkeys % t == 0)
    nheads = KV_HEADS * Q_GROUP
    return pl.pallas_call(
        _att_dense_kernel,
        grid=(dec_batch, nblk, nkeys // tk),
        in_specs=[pl.BlockSpec((KV_HEADS, Q_GROUP, LANES, tq),
                               lambda b, i, k: (0, 0, 0, base + b * nblk + i)),
                  pl.BlockSpec((None, KV_HEADS, tk, LANES), lambda b, i, k: (b, 0, k, 0)),
                  pl.BlockSpec((None, KV_HEADS, LANES, tk), lambda b, i, k: (b, 0, 0, k))],
        out_specs=pl.BlockSpec((tq, Q_WIDTH), lambda b, i, k: (b * nblk + i, 0)),
        out_shape=jax.ShapeDtypeStruct((dec_batch * dec_seq, Q_WIDTH), BF16),
        scratch_shapes=[pltpu.VMEM((nheads, 1, tq), F32), pltpu.VMEM((nheads, 1, tq), F32),
                        pltpu.VMEM((nheads, LANES, tq), F32)],
        compiler_params=_cparams(("parallel", "parallel", "arbitrary")),
        name="att_dense",
    )(qt, k_all, vt_all)


def _lru_kernel(xf_ref, xfp_ref, xfn_ref, xb_ref, xbp_ref, xbn_ref, cw_ref, cb_ref,
                wa_hi_ref, wa_lo_ref, wx_hi_ref, wx_lo_ref, ba_ref, bx_ref, lam_ref,
                h0f_ref, h0b_ref, hf_ref, hb_ref,
                af_scr, bf_scr, ab_scr, bb_scr, sf_scr, sb_scr, *, nctx_blk, seq_blk):
    i = pl.program_id(0)
    nchunk = pl.num_programs(0)
    ts = xf_ref.shape[0]
    row = lax.broadcasted_iota(jnp.int32, (ts, LRU_WIDTH), 0)
    tile_row = lax.broadcasted_iota(jnp.int32, (SUBLANES, LRU_WIDTH), 0)

    def chunk_info(c):
        lat = c >= nctx_blk
        pos = (c - nctx_blk) % seq_blk
        starts = jnp.logical_or(jnp.logical_not(lat), pos == 0)
        ends = jnp.logical_or(jnp.logical_not(lat), pos == seq_blk - 1)
        group = jnp.where(lat, 1 + (c - nctx_blk) // seq_blk, 0)
        return starts, ends, group

    def gates(x_ref, xp_ref, xn_ref, starts, ends, d, a_scr, b_scr):
        x = x_ref[...]
        prev = jnp.where(starts, 0.0, xp_ref[SUBLANES - 1:SUBLANES, :])
        nxt0 = jnp.where(ends, 0.0, xn_ref[0:1, :])
        nxt1 = jnp.where(ends, 0.0, xn_ref[1:2, :])
        x_m1 = jnp.where(row == 0, prev, pltpu.roll(x, 1, 0))
        x_p1 = jnp.where(row == ts - 1, nxt0, pltpu.roll(x, ts - 1, 0))
        x_p2 = jnp.where(row == ts - 2, nxt0, jnp.where(row == ts - 1, nxt1, pltpu.roll(x, ts - 2, 0)))
        xc = (cw_ref[0:1, :] * x_m1 + cw_ref[1:2, :] * x + cw_ref[2:3, :] * x_p1
              + cw_ref[3:4, :] * x_p2 + cb_ref[...])
        xc_hi, xc_lo = _split(xc)

        def blockdiag(hi_ref, lo_ref):
            return _dot(xc_hi, hi_ref[d]) + _dot(xc_lo, hi_ref[d]) + _dot(xc_hi, lo_ref[d])

        r = _sigmoid(blockdiag(wa_hi_ref, wa_lo_ref) + ba_ref[d])
        g = _sigmoid(blockdiag(wx_hi_ref, wx_lo_ref) + bx_ref[d])
        neg_lam = -lam_ref[d]
        softplus = jnp.maximum(neg_lam, 0.0) + jnp.log1p(jnp.exp(-jnp.abs(neg_lam)))
        log_a = -LRU_C * r * softplus
        a = jnp.exp(log_a)
        a_scr[...] = a
        b_scr[...] = jnp.sqrt(1.0 - a * a) * (g * xc)

    f_starts, f_ends, f_group = chunk_info(i)
    cb = nchunk - 1 - i
    b_starts, b_ends, b_group = chunk_info(cb)
    gates(xf_ref, xfp_ref, xfn_ref, f_starts, f_ends, 0, af_scr, bf_scr)
    gates(xb_ref, xbp_ref, xbn_ref, b_starts, b_ends, 1, ab_scr, bb_scr)

    @pl.when(f_starts)
    def _():
        sf_scr[...] = jnp.broadcast_to(h0f_ref[pl.ds(f_group, 1), :], sf_scr.shape)

    @pl.when(b_ends)
    def _():
        sb_scr[...] = jnp.broadcast_to(h0b_ref[pl.ds(b_group, 1), :], sb_scr.shape)

    ntile = ts // SUBLANES

    def tile_step(k, carry):
        hf, hb = carry
        base_f = pl.multiple_of(k * SUBLANES, SUBLANES)
        base_b = pl.multiple_of((ntile - 1 - k) * SUBLANES, SUBLANES)
        a_f = af_scr[pl.ds(base_f, SUBLANES), :]
        b_f = bf_scr[pl.ds(base_f, SUBLANES), :]
        a_b = ab_scr[pl.ds(base_b, SUBLANES), :]
        b_b = bb_scr[pl.ds(base_b, SUBLANES), :]
        out_f = jnp.zeros((SUBLANES, LRU_WIDTH), F32)
        out_b = jnp.zeros((SUBLANES, LRU_WIDTH), F32)
        for r in range(SUBLANES):
            rb = SUBLANES - 1 - r
            hf = a_f[r:r + 1, :] * hf + b_f[r:r + 1, :]
            hb = a_b[rb:rb + 1, :] * hb + b_b[rb:rb + 1, :]
            out_f = jnp.where(tile_row == r, hf, out_f)
            out_b = jnp.where(tile_row == rb, hb, out_b)
        hf_ref[pl.ds(base_f, SUBLANES), :] = out_f
        hb_ref[pl.ds(base_b, SUBLANES), :] = out_b
        return hf, hb

    hf, hb = lax.fori_loop(0, ntile, tile_step, (sf_scr[0:1, :], sb_scr[0:1, :]))
    sf_scr[...] = jnp.broadcast_to(hf, sf_scr.shape)
    sb_scr[...] = jnp.broadcast_to(hb, sb_scr.shape)


def _lru(z1, conv_w, conv_b, wa_hi, wa_lo, wx_hi, wx_lo, ba, bx, lam, h0f, h0b, *, nctx, dec_seq):
    n = z1.shape[0]
    ts = TOKEN_TILE
    nchunk = n // ts
    xcol = COL_XR // LRU_WIDTH
    tiles = ts // SUBLANES
    last_tile = n // SUBLANES - 1

    def cur(rev):
        return lambda i: ((nchunk - 1 - i) if rev else i, xcol)

    def prev(rev):
        return lambda i: (jnp.maximum(((nchunk - 1 - i) if rev else i) * tiles - 1, 0), xcol)

    def nxt(rev):
        return lambda i: (jnp.minimum((((nchunk - 1 - i) if rev else i) + 1) * tiles, last_tile), xcol)

    x_specs = []
    for rev in (False, True):
        x_specs += [pl.BlockSpec((ts, LRU_WIDTH), cur(rev)),
                    pl.BlockSpec((SUBLANES, LRU_WIDTH), prev(rev)),
                    pl.BlockSpec((SUBLANES, LRU_WIDTH), nxt(rev))]
    full = lambda shape: pl.BlockSpec(shape, lambda i: (0,) * len(shape))
    w_spec = full((2, LRU_WIDTH, LRU_WIDTH))
    v_spec = full((2, 1, LRU_WIDTH))
    out_shape = jax.ShapeDtypeStruct((n, LRU_WIDTH), F32)
    scr = pltpu.VMEM((ts, LRU_WIDTH), F32)
    state = pltpu.VMEM((SUBLANES, LRU_WIDTH), F32)
    return pl.pallas_call(
        functools.partial(_lru_kernel, nctx_blk=nctx // ts, seq_blk=dec_seq // ts),
        grid=(nchunk,),
        in_specs=x_specs + [full((4, LRU_WIDTH)), full((1, LRU_WIDTH)),
                            w_spec, w_spec, w_spec, w_spec, v_spec, v_spec, v_spec,
                            full((SUBLANES, LRU_WIDTH)), full((SUBLANES, LRU_WIDTH))],
        out_specs=[pl.BlockSpec((ts, LRU_WIDTH), lambda i: (i, 0)),
                   pl.BlockSpec((ts, LRU_WIDTH), lambda i: (nchunk - 1 - i, 0))],
        out_shape=[out_shape, out_shape],
        scratch_shapes=[scr, scr, scr, scr, state, state],
        compiler_params=_cparams(("arbitrary",)),
        name="lru",
    )(z1, z1, z1, z1, z1, z1, conv_w, conv_b, wa_hi, wa_lo, wx_hi, wx_lo, ba, bx, lam, h0f, h0b)


def _merge_kernel(x_ref, mod_ref, oa_ref, ob_ref, hf_ref, hb_ref, z2_ref,
                  woa_ref, wob_ref, woc_ref, wout_ref, o_ref):
    d = x_ref.shape[1]
    oc = (hf_ref[...] + hb_ref[...]) * _gelu(z2_ref[:, 0:LRU_WIDTH])
    ga = z2_ref[:, LRU_WIDTH:LRU_WIDTH + d]
    gb = z2_ref[:, LRU_WIDTH + d:LRU_WIDTH + 2 * d]
    gc = z2_ref[:, LRU_WIDTH + 2 * d:LRU_WIDTH + 3 * d]
    merged = (_sigmoid(ga) * _dot(oa_ref[...], woa_ref[...])
              + _sigmoid(gb) * _dot(ob_ref[...], wob_ref[...])
              + _sigmoid(gc) * _dot(oc.astype(BF16), woc_ref[...]))
    out = _dot(merged.astype(BF16), wout_ref[...])
    gate = mod_ref[0][:, 2 * d:3 * d]
    o_ref[...] = x_ref[...] + gate * out


def _merge(x, mod3, oa, ob, hf, hb, z2, wo_a, wo_b, wo_c, w_out, *, nctx, dec_seq):
    n, d = x.shape
    tm = TOKEN_TILE
    row = lambda w: pl.BlockSpec((tm, w), lambda i: (i, 0))
    full = lambda a: pl.BlockSpec(a.shape, lambda i: (0, 0))
    return pl.pallas_call(
        _merge_kernel,
        grid=(n // tm,),
        in_specs=[row(d),
                  pl.BlockSpec((1, 1, mod3.shape[2]),
                               lambda i: (_group_of_block(i, tm, nctx, dec_seq), 0, 0)),
                  row(Q_WIDTH), row(Q_WIDTH), row(LRU_WIDTH), row(LRU_WIDTH), row(Z2_WIDTH),
                  full(wo_a), full(wo_b), full(wo_c), full(w_out)],
        out_specs=row(d),
        out_shape=jax.ShapeDtypeStruct((n, d), F32),
        compiler_params=_cparams(("parallel",)),
        name="merge",
    )(x, mod3, oa, ob, hf, hb, z2, wo_a, wo_b, wo_c, w_out)


ORDER_SENTINEL = 1e9


def _candidate_groups():
    k, s = PEER_TOPK, SUBLANES
    split = 4
    groups, covered = [], set()
    for a in range(split):
        for b0 in range(0, k // (a + 1), s):
            groups.append((True, a, b0, 0))
            covered |= {(a, b) for b in range(b0, b0 + s)}
    for b in range(k // (split + 1)):
        for a0 in range(0, k // (b + 1), s):
            if a0 + s > split:
                groups.append((False, a0, b, max(split, a0)))
                covered |= {(a, b) for a in range(max(split, a0), a0 + s)}
    needed = {(a, b) for a in range(k) for b in range(k) if (a + 1) * (b + 1) <= k}
    assert needed <= covered and len(covered) == sum(
        s if f else a0 + s - amin for f, a0, _, amin in groups)
    return groups


def _top_rows(s, k, extras=(), order=None):
    rows = lax.broadcasted_iota(jnp.int32, s.shape, 0).astype(F32) if order is None else order
    vals, idxs, picked = [], [], [[] for _ in extras]
    for _ in range(k):
        m = jnp.max(s, axis=0, keepdims=True)
        pos = jnp.min(jnp.where(s == m, rows, ORDER_SENTINEL), axis=0, keepdims=True)
        sel = rows == pos
        vals.append(m)
        idxs.append(pos)
        for dst, ex in zip(picked, extras):
            dst.append(jnp.max(jnp.where(sel, ex, -1.0), axis=0, keepdims=True))
        s = jnp.where(sel, -jnp.inf, s)
    cat = lambda xs: jnp.concatenate(xs, axis=0)
    return cat(vals), cat(idxs), [cat(p) for p in picked]


def _peer_topk_kernel(q_ref, khi_ref, klo_ref, g_ref, i1_ref, i2_ref):
    for t0 in range(0, q_ref.shape[0], LANES):
        g, e1, e2 = _peer_topk_tile(q_ref, khi_ref, klo_ref, t0)
        g_ref[0, :, t0:t0 + LANES] = g
        i1_ref[0, :, t0:t0 + LANES] = e1
        i2_ref[0, :, t0:t0 + LANES] = e2


def _peer_topk_tile(q_ref, khi_ref, klo_ref, t0):
    top_s, top_i = [], []
    for p in range(2):
        q_hi, q_lo = _split(q_ref[t0:t0 + LANES, p * PEER_DKEY:(p + 1) * PEER_DKEY])
        k_hi, k_lo = khi_ref[0, p], klo_ref[0, p]
        s = _dot_nt(k_hi, q_hi) + _dot_nt(k_lo, q_hi) + _dot_nt(k_hi, q_lo)
        v, ix, _ = _top_rows(s, PEER_TOPK)
        top_s.append(v)
        top_i.append(ix)
    tb = top_s[0].shape[1]
    gshape = (SUBLANES, tb)
    sub_row = lax.broadcasted_iota(jnp.int32, gshape, 0).astype(F32)
    cand_s, cand_1, cand_2, order = [], [], [], []
    for fixed_first, a, b, a_min in _candidate_groups():
        if fixed_first:
            bcast = lambda x: jnp.broadcast_to(x[a:a + 1], gshape)
            cand_s.append(bcast(top_s[0]) + top_s[1][b:b + SUBLANES])
            cand_1.append(bcast(top_i[0]))
            cand_2.append(top_i[1][b:b + SUBLANES])
            order.append(float(a * PEER_TOPK + b) + sub_row)
        else:
            bcast = lambda x: jnp.broadcast_to(x[b:b + 1], gshape)
            live = sub_row >= float(a_min - a)
            cand_s.append(jnp.where(live, top_s[0][a:a + SUBLANES] + bcast(top_s[1]), -jnp.inf))
            cand_1.append(top_i[0][a:a + SUBLANES])
            cand_2.append(bcast(top_i[1]))
            order.append(jnp.where(live, float(a * PEER_TOPK + b) + float(PEER_TOPK) * sub_row,
                                   ORDER_SENTINEL))
    cat = lambda xs: jnp.concatenate(xs, axis=0)
    best, _, (e1, e2) = _top_rows(cat(cand_s), PEER_TOPK, (cat(cand_1), cat(cand_2)), cat(order))
    ex = jnp.exp(best - best[0:1])
    return ex / jnp.sum(ex, axis=0, keepdims=True), e1, e2


def _peer_topk(q, keys_hi, keys_lo):
    n = q.shape[0]
    tb = PEER_TB
    spec = pl.BlockSpec((1, PEER_TOPK, tb), lambda i, h: (h, 0, i))
    shape = jax.ShapeDtypeStruct((PEER_HEADS, PEER_TOPK, n), F32)
    k_spec = pl.BlockSpec((1, 2, PEER_NKEYS, PEER_DKEY), lambda i, h: (h, 0, 0, 0))
    return pl.pallas_call(
        _peer_topk_kernel,
        grid=(n // tb, PEER_HEADS),
        in_specs=[pl.BlockSpec((tb, 2 * PEER_DKEY), lambda i, h: (i, h)), k_spec, k_spec],
        out_specs=[spec, spec, spec],
        out_shape=[shape, shape, shape],
        compiler_params=_cparams(("parallel", "parallel")),
        name="peer_topk",
    )(q, keys_hi, keys_lo)


def _peer_expert_kernel(x_ref, mod_ref, gn_ref, g_ref, i1_ref, i2_ref, ut_ref, v_ref, o_ref,
                        h_scr, w_scr, p_scr, acc_scr):
    c = pl.program_id(1)
    tb, d = x_ref.shape
    ec = ut_ref.shape[1]
    sub = ec // PEER_NKEYS
    npair = PEER_HEADS * PEER_TOPK

    @pl.when(c == 0)
    def _():
        x = x_ref[...]
        y = x * lax.rsqrt(jnp.mean(x * x, axis=-1, keepdims=True) + EPS) * gn_ref[...]
        m = mod_ref[0]
        h_scr[...] = (y * (1.0 + m[:, 4 * d:5 * d]) + m[:, 3 * d:4 * d]).astype(BF16)
        acc_scr[...] = jnp.zeros(acc_scr.shape, F32)
        key_row = lax.broadcasted_iota(jnp.int32, (PEER_NKEYS, npair), 0).astype(F32)

        def token_group(tg, carry):
            for u in range(PEER_TOKEN_UNROLL):
                t = tg * PEER_TOKEN_UNROLL + u
                gate = g_ref[pl.ds(t, 1), :]
                ga = jnp.where(key_row == i1_ref[pl.ds(t, 1), :], gate, 0.0).astype(BF16)
                ob = jnp.where(key_row == i2_ref[pl.ds(t, 1), :], 1.0, 0.0).astype(BF16)
                w_scr[pl.ds(pl.multiple_of(t * PEER_W_PITCH, SUBLANES), PEER_NKEYS), :] = _dot_nt(ga, ob)
            return carry

        lax.fori_loop(0, tb // PEER_TOKEN_UNROLL, token_group, 0)

    h1 = _dot(h_scr[...], ut_ref[...])
    for s in range(sub):
        w = w_scr[pl.ds(c * sub + s, tb, stride=PEER_W_PITCH), :]
        p_scr[:, s * PEER_NKEYS:(s + 1) * PEER_NKEYS] = (
            w * _gelu(h1[:, s * PEER_NKEYS:(s + 1) * PEER_NKEYS])).astype(BF16)
    acc_scr[...] += _dot(p_scr[...], v_ref[...])

    @pl.when(c == pl.num_programs(1) - 1)
    def _():
        o_ref[...] = x_ref[...] + mod_ref[0][:, 5 * d:6 * d] * acc_scr[...]


def _peer_expert(x, mod3, gain, g, i1, i2, ut, v, *, nctx, dec_seq):
    n, d = x.shape
    tb, ec = PEER_TB, PEER_EC
    nexp = v.shape[0]
    npair = PEER_HEADS * PEER_TOPK
    row = lambda w: pl.BlockSpec((tb, w), lambda i, c: (i, 0))
    return pl.pallas_call(
        _peer_expert_kernel,
        grid=(n // tb, nexp // ec),
        in_specs=[row(d),
                  pl.BlockSpec((1, 1, mod3.shape[2]),
                               lambda i, c: (_group_of_block(i, tb, nctx, dec_seq), 0, 0)),
                  pl.BlockSpec((1, d), lambda i, c: (0, 0)),
                  row(npair), row(npair), row(npair),
                  pl.BlockSpec((d, ec), lambda i, c: (0, c)),
                  pl.BlockSpec((ec, d), lambda i, c: (c, 0))],
        out_specs=row(d),
        out_shape=jax.ShapeDtypeStruct((n, d), F32),
        scratch_shapes=[pltpu.VMEM((tb, d), BF16),
                        pltpu.VMEM((tb * PEER_W_PITCH, PEER_NKEYS), F32),
                        pltpu.VMEM((tb, ec), BF16),
                        pltpu.VMEM((tb, d), F32)],
        compiler_params=_cparams(("parallel", "arbitrary")),
        name="peer_expert",
    )(x, mod3, gain.reshape(1, d), g, i1, i2, ut, v)


def _final_norm_kernel(x_ref, g_ref, o_ref):
    x = x_ref[...]
    o_ref[...] = x * lax.rsqrt(jnp.mean(x * x, axis=-1, keepdims=True) + EPS) * g_ref[...]


def _final_norm(x, gain):
    n, d = x.shape
    tm = 512
    return pl.pallas_call(
        _final_norm_kernel,
        grid=(n // tm,),
        in_specs=[pl.BlockSpec((tm, d), lambda i: (i, 0)), pl.BlockSpec((1, d), lambda i: (0, 0))],
        out_specs=pl.BlockSpec((tm, d), lambda i: (i, 0)),
        out_shape=jax.ShapeDtypeStruct((n, d), F32),
        compiler_params=_cparams(("parallel",)),
        name="final_norm",
    )(x, gain.reshape(1, d))


def _rope_tables(dec_seq, tile):
    t = jnp.arange(dec_seq)
    n_freq = HEAD_DIM // 4
    inv = ROPE_BASE ** (-jnp.arange(n_freq, dtype=F32) / n_freq)
    ang = jnp.concatenate([(t // GRID_W).astype(F32)[:, None] * inv,
                           (t % GRID_W).astype(F32)[:, None] * inv], axis=-1)
    cos, sin = jnp.cos(ang), jnp.sin(ang)
    reps = LANES // HEAD_DIM
    cos_t = jnp.tile(jnp.concatenate([cos, cos], axis=-1), (1, reps))
    sin_t = jnp.tile(jnp.concatenate([-sin, sin], axis=-1), (1, reps))
    cos_t = jnp.concatenate([jnp.ones((tile, LANES), F32), cos_t], axis=0)
    sin_t = jnp.concatenate([jnp.zeros((tile, LANES), F32), sin_t], axis=0)
    return cos_t, sin_t


def _dup_cache(cache):
    c = jnp.transpose(cache, (0, 2, 1, 3))
    return jnp.concatenate([c, c], axis=-1).astype(BF16)


def _dense_blockdiag(w):
    dirs, nb, bw, _ = w.shape
    eye = jnp.eye(nb, dtype=w.dtype)
    return jnp.einsum('dncf,nm->dncmf', w, eye).reshape(dirs, nb * bw, nb * bw)


def _pad_rows(a, rows):
    return jnp.concatenate([a, jnp.zeros((rows - a.shape[0],) + a.shape[1:], a.dtype)], axis=0)


def kernel(x_prompt, x_sample, c, cache_wa_k, cache_wa_v, cache_ax_k, cache_ax_v, state_lru_fwd,
           state_lru_bwd, c_ctx, w_ada, b_ada, g_norm1, w_in, wa_sink, ax_q_gain, ax_k_gain, conv_w,
           conv_b, lru_wa, lru_ba, lru_wx, lru_bx, lru_lambda, wo_a, wo_b, wo_c, w_out, g_norm2,
           peer_wq, peer_keys, peer_u, peer_v, g_final):
    batch, seq, d = x_prompt.shape
    dec_batch, dec_seq, _ = x_sample.shape
    depth = w_in.shape[0]
    nctx = batch * seq
    nlat = dec_batch * dec_seq
    assert seq == TOKEN_TILE and dec_seq % TOKEN_TILE == 0 and 1 + dec_batch <= SUBLANES
    assert w_in.shape[2] == Z1_WIDTH + Z2_WIDTH and d == 1024
    sizes = dict(nctx=nctx, dec_seq=dec_seq)

    x = jnp.concatenate([x_prompt.reshape(nctx, d), x_sample.reshape(nlat, d)], axis=0)
    cvec = _pad_rows(jnp.concatenate([c_ctx[None, :], c], axis=0), SUBLANES)
    mods = _ada(cvec, w_ada, b_ada)

    cos_tab, sin_tab = _rope_tables(dec_seq, TOKEN_TILE)
    head_mean = jnp.kron(jnp.eye(LANES // HEAD_DIM, dtype=F32),
                         jnp.full((HEAD_DIM, HEAD_DIM), 1.0 / HEAD_DIM, F32)).astype(BF16)
    reps = LANES // HEAD_DIM

    new_wa_k, new_wa_v, new_ax_k, new_ax_v, new_hf, new_hb = [], [], [], [], [], []
    for l in range(depth):
        mod3 = mods[l].reshape(SUBLANES, 1, 6 * d)
        w_in_bf = w_in[l].astype(BF16)
        z1 = _modmm(x, mod3, g_norm1[l], w_in_bf[:, :Z1_WIDTH], None, shift_col=0, tn=512, **sizes)
        z2 = _modmm(x, mod3, g_norm1[l], w_in_bf[:, Z1_WIDTH:], None, shift_col=0, tn=512, **sizes)

        gq = jnp.tile(ax_q_gain[l], reps).reshape(1, LANES)
        gk = jnp.tile(ax_k_gain[l], reps).reshape(1, LANES)
        qsa, kka, vva, qsb, kkb, vvb, kbn, qtb = _prep(z1, cos_tab, sin_tab, gq, gk, head_mean, **sizes)

        oa_ctx, ob_ctx = _att_ctx(wa_sink[l], qsa, kka, vva, qsb, kkb, vvb, nseq=batch, seq=seq)
        oa_lat = _att_win(wa_sink[l], qsa, kka, vva, _dup_cache(cache_wa_k[:, l]),
                          _dup_cache(cache_wa_v[:, l]), nctx=nctx, dec_batch=dec_batch, dec_seq=dec_seq)

        def with_cache(cache, cur):
            lat = cur[:, nctx:].reshape(KV_HEADS, dec_batch, dec_seq, LANES)
            return jnp.concatenate([_dup_cache(cache), jnp.transpose(lat, (1, 0, 2, 3))], axis=2)

        ob_lat = _att_dense(qtb, with_cache(cache_ax_k[:, l], kkb),
                            jnp.swapaxes(with_cache(cache_ax_v[:, l], vvb), 2, 3),
                            nctx=nctx, dec_batch=dec_batch, dec_seq=dec_seq)
        oa = jnp.concatenate([oa_ctx, oa_lat], axis=0)
        ob = jnp.concatenate([ob_ctx, ob_lat], axis=0)

        wa_hi, wa_lo = _split(_dense_blockdiag(lru_wa[l]))
        wx_hi, wx_lo = _split(_dense_blockdiag(lru_wx[l]))
        vec = lambda a: a.reshape(2, 1, LRU_WIDTH)
        h0f = _pad_rows(jnp.concatenate([jnp.zeros((1, LRU_WIDTH), F32), state_lru_fwd[:, l]], axis=0), SUBLANES)
        h0b = _pad_rows(jnp.concatenate([jnp.zeros((1, LRU_WIDTH), F32), state_lru_bwd[:, l]], axis=0), SUBLANES)
        hf, hb = _lru(z1, conv_w[l], conv_b[l].reshape(1, LRU_WIDTH), wa_hi, wa_lo, wx_hi, wx_lo,
                      vec(lru_ba[l]), vec(lru_bx[l]), vec(lru_lambda[l]), h0f, h0b, **sizes)

        x = _merge(x, mod3, oa, ob, hf, hb, z2, wo_a[l].astype(BF16), wo_b[l].astype(BF16),
                   wo_c[l].astype(BF16), w_out[l].astype(BF16), **sizes)

        wq_hi, wq_lo = _split(peer_wq[l])
        q = _modmm(x, mod3, g_norm2[l], wq_hi, wq_lo, shift_col=3 * d, tn=512, **sizes)
        keys_hi, keys_lo = _split(peer_keys[l])
        g, i1, i2 = _peer_topk(q, keys_hi, keys_lo)
        pairs = lambda a: jnp.transpose(a, (2, 0, 1)).reshape(nctx + nlat, PEER_HEADS * PEER_TOPK)
        x = _peer_expert(x, mod3, g_norm2[l], pairs(g), pairs(i1), pairs(i2),
                         peer_u[l].T.astype(BF16), peer_v[l].astype(BF16), **sizes)

        ctx4 = lambda a: a[:nctx].reshape(batch, seq, KV_HEADS, HEAD_DIM)
        new_wa_k.append(ctx4(z1[:, COL_KA:COL_KA + KV_WIDTH]))
        new_wa_v.append(ctx4(z1[:, COL_VA:COL_VA + KV_WIDTH]))
        new_ax_k.append(ctx4(kbn))
        new_ax_v.append(ctx4(z1[:, COL_VB:COL_VB + KV_WIDTH]))
        new_hf.append(hf[:nctx].reshape(batch, seq, LRU_WIDTH)[:, -1])
        new_hb.append(hb[:nctx].reshape(batch, seq, LRU_WIDTH)[:, 0])

    y = _final_norm(x, g_final)
    stack = lambda xs: jnp.stack(xs, axis=1)
    return (y[:nctx].reshape(batch, seq, d), y[nctx:].reshape(dec_batch, dec_seq, d),
            stack(new_wa_k), stack(new_wa_v), stack(new_ax_k), stack(new_ax_v),
            stack(new_hf), stack(new_hb))
```

```python
import functools

import jax
import jax.numpy as jnp
from jax import lax
from jax.experimental import pallas as pl
from jax.experimental.pallas import tpu as pltpu

F32 = jnp.float32
BF16 = jnp.bfloat16

HEAD_DIM = 64
KV_HEADS = 2
Q_GROUP = 4
Q_WIDTH = KV_HEADS * Q_GROUP * HEAD_DIM
KV_WIDTH = KV_HEADS * HEAD_DIM
WINDOW = 128
GRID_W = 64
LRU_WIDTH = 512
LRU_BLOCKS = 8
LRU_C = 8.0
PEER_HEADS = 8
PEER_NKEYS = 128
PEER_DKEY = 128
PEER_TOPK = 16
ROPE_BASE = 10000.0
EPS = 1e-6
NEG = -1e30
LANES = 128
SUBLANES = 8
VMEM_LIMIT = 56 * 1024 * 1024

Z1_WIDTH = 2048
Z2_WIDTH = 3584
COL_KA, COL_VA, COL_QB, COL_KB, COL_VB, COL_XR = 512, 640, 768, 1280, 1408, 1536

TOKEN_TILE = 256
ATT_A_TQ = 128
ATT_B_TQ = 256
ATT_B_TK = (1536, 1024, 512)
PEER_TB = 256
PEER_EC = 2048
PEER_TOKEN_UNROLL = 16
PEER_W_PITCH = PEER_NKEYS + SUBLANES


def _cparams(sem):
    return pltpu.CompilerParams(dimension_semantics=sem, vmem_limit_bytes=VMEM_LIMIT)


def _split(x):
    hi = x.astype(BF16)
    lo = (x - hi.astype(F32)).astype(BF16)
    return hi, lo


def _dot(a, b):
    return jnp.dot(a, b, preferred_element_type=F32)


def _dot_nt(a, b):
    return lax.dot_general(a, b, (((1,), (1,)), ((), ())), preferred_element_type=F32)


def _sigmoid(x):
    return 1.0 / (1.0 + jnp.exp(-x))


def _gelu(x):
    return 0.5 * x * (1.0 + jnp.tanh(0.7978845608028654 * (x + 0.044715 * (x * x * x))))


def _group_of_block(i, tile, nctx, dec_seq):
    row = i * tile
    return jnp.where(row < nctx, 0, 1 + (row - nctx) // dec_seq)


def _ada_kernel(c_ref, w_ref, b_ref, o_ref):
    c = c_ref[...]
    s = c * _sigmoid(c)
    s_hi, s_lo = _split(s)
    w_hi, w_lo = _split(w_ref[0])
    o_ref[0] = _dot(s_hi, w_hi) + _dot(s_lo, w_hi) + _dot(s_hi, w_lo) + b_ref[0]


def _ada(cvec, w_ada, b_ada):
    depth, d, e = w_ada.shape
    tn = 1536
    return pl.pallas_call(
        _ada_kernel,
        grid=(depth, e // tn),
        in_specs=[
            pl.BlockSpec((SUBLANES, d), lambda l, j: (0, 0)),
            pl.BlockSpec((1, d, tn), lambda l, j: (l, 0, j)),
            pl.BlockSpec((1, 1, tn), lambda l, j: (l, 0, j)),
        ],
        out_specs=pl.BlockSpec((1, SUBLANES, tn), lambda l, j: (l, 0, j)),
        out_shape=jax.ShapeDtypeStruct((depth, SUBLANES, e), F32),
        compiler_params=_cparams(("parallel", "parallel")),
        name="ada",
    )(cvec, w_ada, b_ada.reshape(depth, 1, e))


def _modmm_kernel(x_ref, mod_ref, g_ref, w_ref, *rest, shift_col, three_pass):
    d = x_ref.shape[1]
    if three_pass:
        wlo_ref, o_ref, h_scr, hlo_scr = rest
    else:
        o_ref, h_scr = rest

    @pl.when(pl.program_id(1) == 0)
    def _():
        x = x_ref[...]
        y = x * lax.rsqrt(jnp.mean(x * x, axis=-1, keepdims=True) + EPS) * g_ref[...]
        m = mod_ref[0]
        h = y * (1.0 + m[:, shift_col + d:shift_col + 2 * d]) + m[:, shift_col:shift_col + d]
        hi = h.astype(BF16)
        h_scr[...] = hi
        if three_pass:
            hlo_scr[...] = (h - hi.astype(F32)).astype(BF16)

    acc = _dot(h_scr[...], w_ref[...])
    if three_pass:
        acc = acc + _dot(hlo_scr[...], w_ref[...]) + _dot(h_scr[...], wlo_ref[...])
    o_ref[...] = acc


def _modmm(x, mod3, gain, w, w_lo, *, shift_col, tn, nctx, dec_seq):
    n, d = x.shape
    width = w.shape[1]
    tm = 512
    three_pass = w_lo is not None
    grp = lambda i, j: (_group_of_block(i, tm, nctx, dec_seq), 0, 0)
    in_specs = [
        pl.BlockSpec((tm, d), lambda i, j: (i, 0)),
        pl.BlockSpec((1, 1, mod3.shape[2]), grp),
        pl.BlockSpec((1, d), lambda i, j: (0, 0)),
        pl.BlockSpec((d, tn), lambda i, j: (0, j)),
    ]
    args = [x, mod3, gain.reshape(1, d), w]
    scratch = [pltpu.VMEM((tm, d), BF16)]
    if three_pass:
        in_specs.append(pl.BlockSpec((d, tn), lambda i, j: (0, j)))
        args.append(w_lo)
        scratch.append(pltpu.VMEM((tm, d), BF16))
    return pl.pallas_call(
        functools.partial(_modmm_kernel, shift_col=shift_col, three_pass=three_pass),
        grid=(n // tm, width // tn),
        in_specs=in_specs,
        out_specs=pl.BlockSpec((tm, tn), lambda i, j: (i, j)),
        out_shape=jax.ShapeDtypeStruct((n, width), F32),
        scratch_shapes=scratch,
        compiler_params=_cparams(("parallel", "arbitrary")),
        name="modmm",
    )(*args)


def _prep_kernel(z_ref, cos_ref, sin_ref, gq_ref, gk_ref, m_ref,
                 qsa_ref, kka_ref, vva_ref, qsb_ref, kkb_ref, vvb_ref, kbn_ref, qtb_ref):
    tm = z_ref.shape[0]
    lane = lax.broadcasted_iota(jnp.int32, (tm, LANES), 1)
    low_head = lane < HEAD_DIM
    first_half = (lane & (HEAD_DIM // 2)) == 0
    cos = cos_ref[...]
    sin = sin_ref[...]
    mmat = m_ref[...]
    scale = HEAD_DIM ** -0.5

    def rope(x):
        back = pltpu.roll(x, HEAD_DIM // 2, 1)
        fwd = pltpu.roll(x, LANES - HEAD_DIM // 2, 1)
        return x * cos + jnp.where(first_half, fwd, back) * sin

    def headnorm(x, gain):
        sq_hi, sq_lo = _split(x * x)
        ms = _dot(sq_hi, mmat) + _dot(sq_lo, mmat)
        return x * lax.rsqrt(ms + EPS) * gain

    def store_q(ref, c, q, t_ref=None):
        zero = jnp.zeros_like(q)
        j, g0 = c // 2, 2 * (c % 2)
        for g, qm in ((g0, jnp.where(low_head, q, zero)), (g0 + 1, jnp.where(low_head, zero, q))):
            ref[j, g] = qm.astype(BF16)
            if t_ref is not None:
                t_ref[j, g] = qm.T.astype(BF16)

    def store_dup(ref, x):
        swapped = pltpu.roll(x, HEAD_DIM, 1)
        ref[0] = jnp.where(low_head, x, swapped).astype(BF16)
        ref[1] = jnp.where(low_head, swapped, x).astype(BF16)

    for c in range(Q_WIDTH // LANES):
        store_q(qsa_ref, c, rope(z_ref[:, c * LANES:(c + 1) * LANES]) * scale)
        qb = headnorm(z_ref[:, COL_QB + c * LANES:COL_QB + (c + 1) * LANES], gq_ref[...])
        store_q(qsb_ref, c, rope(qb) * scale, qtb_ref)
    store_dup(kka_ref, rope(z_ref[:, COL_KA:COL_KA + KV_WIDTH]))
    store_dup(vva_ref, z_ref[:, COL_VA:COL_VA + KV_WIDTH])
    kb = rope(headnorm(z_ref[:, COL_KB:COL_KB + KV_WIDTH], gk_ref[...]))
    kbn_ref[...] = kb
    store_dup(kkb_ref, kb)
    store_dup(vvb_ref, z_ref[:, COL_VB:COL_VB + KV_WIDTH])


def _prep(z1, cos_tab, sin_tab, gq, gk, mmat, *, nctx, dec_seq):
    n = z1.shape[0]
    tm = TOKEN_TILE
    nctx_blk = nctx // tm
    seq_blk = dec_seq // tm

    def tab_idx(i):
        return (jnp.where(i < nctx_blk, 0, 1 + (i - nctx_blk) % seq_blk), 0)

    qs_spec = pl.BlockSpec((KV_HEADS, Q_GROUP, tm, LANES), lambda i: (0, 0, i, 0))
    kv_spec = pl.BlockSpec((KV_HEADS, tm, LANES), lambda i: (0, i, 0))
    qs_shape = jax.ShapeDtypeStruct((KV_HEADS, Q_GROUP, n, LANES), BF16)
    kv_shape = jax.ShapeDtypeStruct((KV_HEADS, n, LANES), BF16)
    return pl.pallas_call(
        _prep_kernel,
        grid=(n // tm,),
        in_specs=[
            pl.BlockSpec((tm, Z1_WIDTH), lambda i: (i, 0)),
            pl.BlockSpec((tm, LANES), tab_idx),
            pl.BlockSpec((tm, LANES), tab_idx),
            pl.BlockSpec((1, LANES), lambda i: (0, 0)),
            pl.BlockSpec((1, LANES), lambda i: (0, 0)),
            pl.BlockSpec((LANES, LANES), lambda i: (0, 0)),
        ],
        out_specs=[qs_spec, kv_spec, kv_spec, qs_spec, kv_spec, kv_spec,
                   pl.BlockSpec((tm, LANES), lambda i: (i, 0)),
                   pl.BlockSpec((KV_HEADS, Q_GROUP, LANES, tm), lambda i: (0, 0, 0, i))],
        out_shape=[qs_shape, kv_shape, kv_shape, qs_shape, kv_shape, kv_shape,
                   jax.ShapeDtypeStruct((n, LANES), F32),
                   jax.ShapeDtypeStruct((KV_HEADS, Q_GROUP, LANES, n), BF16)],
        compiler_params=_cparams(("parallel",)),
        name="prep",
    )(z1, cos_tab, sin_tab, gq, gk, mmat)


def _merge_heads(o, tq, c):
    lane = lax.broadcasted_iota(jnp.int32, (tq, LANES), 1)
    g = 2 * c
    return jnp.where(lane < HEAD_DIM, o[g * tq:(g + 1) * tq], o[(g + 1) * tq:(g + 2) * tq])


def _sink_column(sink_ref, j, tq):
    return jnp.concatenate(
        [jnp.full((tq, 1), sink_ref[j * Q_GROUP + g], F32) for g in range(Q_GROUP)], axis=0)


def _att_ctx_kernel(sink_ref, qsa_ref, kka_ref, vva_ref, qsb_ref, kkb_ref, vvb_ref, oa_ref, ob_ref):
    tq = qsa_ref.shape[2]
    for q_ref, k_ref, v_ref, o_ref, has_sink in (
            (qsa_ref, kka_ref, vva_ref, oa_ref, True), (qsb_ref, kkb_ref, vvb_ref, ob_ref, False)):
        for j in range(KV_HEADS):
            qs = q_ref[j].reshape(Q_GROUP * tq, LANES)
            s = _dot_nt(qs, k_ref[j])
            m = jnp.max(s, axis=-1, keepdims=True)
            if has_sink:
                sk = _sink_column(sink_ref, j, tq)
                m = jnp.maximum(m, sk)
            e = jnp.exp(s - m)
            den = jnp.sum(e, axis=-1, keepdims=True)
            if has_sink:
                den = den + jnp.exp(sk - m)
            o = _dot(e.astype(BF16), v_ref[j]) / den
            for c in range(2):
                col = (2 * j + c) * LANES
                o_ref[:, col:col + LANES] = _merge_heads(o, tq, c).astype(BF16)


def _att_ctx(sink, qsa, kka, vva, qsb, kkb, vvb, *, nseq, seq):
    qs_spec = pl.BlockSpec((KV_HEADS, Q_GROUP, seq, LANES), lambda i: (0, 0, i, 0))
    kv_spec = pl.BlockSpec((KV_HEADS, seq, LANES), lambda i: (0, i, 0))
    o_spec = pl.BlockSpec((seq, Q_WIDTH), lambda i: (i, 0))
    o_shape = jax.ShapeDtypeStruct((nseq * seq, Q_WIDTH), BF16)
    return pl.pallas_call(
        _att_ctx_kernel,
        grid=(nseq,),
        in_specs=[pl.BlockSpec(memory_space=pltpu.SMEM),
                  qs_spec, kv_spec, kv_spec, qs_spec, kv_spec, kv_spec],
        out_specs=[o_spec, o_spec],
        out_shape=[o_shape, o_shape],
        compiler_params=_cparams(("parallel",)),
        name="att_ctx",
    )(sink, qsa, kka, vva, qsb, kkb, vvb)


def _att_win_kernel(sink_ref, qs_ref, kp_ref, kc_ref, kn_ref, vp_ref, vc_ref, vn_ref,
                    kx_ref, vx_ref, o_ref, *, nblk):
    i = pl.program_id(1)
    tq = qs_ref.shape[2]
    rows = Q_GROUP * tq
    a_idx = lax.broadcasted_iota(jnp.int32, (rows, tq), 0) & (tq - 1)
    j_idx = lax.broadcasted_iota(jnp.int32, (rows, tq), 1)
    valid_prev = (j_idx >= a_idx) & (i > 0)
    valid_next = (j_idx <= a_idx) & (i < nblk - 1)
    for j in range(KV_HEADS):
        qs = qs_ref[j].reshape(rows, LANES)
        s_x = _dot_nt(qs, kx_ref[j])
        s_p = jnp.where(valid_prev, _dot_nt(qs, kp_ref[j]), NEG)
        s_c = _dot_nt(qs, kc_ref[j])
        s_n = jnp.where(valid_next, _dot_nt(qs, kn_ref[j]), NEG)
        sk = _sink_column(sink_ref, j, tq)
        m = jnp.maximum(jnp.max(s_x, axis=-1, keepdims=True), sk)
        for s in (s_p, s_c, s_n):
            m = jnp.maximum(m, jnp.max(s, axis=-1, keepdims=True))
        den = jnp.exp(sk - m)
        o = jnp.zeros((rows, LANES), F32)
        for s, v_ref in ((s_x, vx_ref), (s_p, vp_ref), (s_c, vc_ref), (s_n, vn_ref)):
            e = jnp.exp(s - m)
            den = den + jnp.sum(e, axis=-1, keepdims=True)
            o = o + _dot(e.astype(BF16), v_ref[j])
        o = o / den
        for c in range(2):
            col = (2 * j + c) * LANES
            o_ref[:, col:col + LANES] = _merge_heads(o, tq, c).astype(BF16)


def _att_win(sink, qs, kk, vv, kx, vx, *, nctx, dec_batch, dec_seq):
    tq = ATT_A_TQ
    assert tq == WINDOW
    nblk = dec_seq // tq
    base = nctx // tq
    past = kx.shape[2]

    def q_idx(b, i):
        return (0, 0, base + b * nblk + i, 0)

    def kv_idx(delta):
        return lambda b, i: (0, base + b * nblk + jnp.clip(i + delta, 0, nblk - 1), 0)

    kv_specs = [pl.BlockSpec((KV_HEADS, tq, LANES), kv_idx(d)) for d in (-1, 0, 1)]
    x_spec = pl.BlockSpec((None, KV_HEADS, past, LANES), lambda b, i: (b, 0, 0, 0))
    return pl.pallas_call(
        functools.partial(_att_win_kernel, nblk=nblk),
        grid=(dec_batch, nblk),
        in_specs=[pl.BlockSpec(memory_space=pltpu.SMEM),
                  pl.BlockSpec((KV_HEADS, Q_GROUP, tq, LANES), q_idx)] + kv_specs + kv_specs
                 + [x_spec, x_spec],
        out_specs=pl.BlockSpec((tq, Q_WIDTH), lambda b, i: (b * nblk + i, 0)),
        out_shape=jax.ShapeDtypeStruct((dec_batch * dec_seq, Q_WIDTH), BF16),
        compiler_params=_cparams(("parallel", "parallel")),
        name="att_win",
    )(sink, qs, kk, kk, kk, vv, vv, vv, kx, vx)


def _att_dense_kernel(qt_ref, k_ref, vt_ref, o_ref, m_scr, l_scr, acc_scr):
    kc = pl.program_id(2)
    tq = qt_ref.shape[3]

    @pl.when(kc == 0)
    def _():
        m_scr[...] = jnp.full(m_scr.shape, -jnp.inf, F32)
        l_scr[...] = jnp.zeros(l_scr.shape, F32)
        acc_scr[...] = jnp.zeros(acc_scr.shape, F32)

    scores = [_dot(k_ref[j], qt_ref[j, g]) for j in range(KV_HEADS) for g in range(Q_GROUP)]
    for j in range(KV_HEADS):
        for g in range(Q_GROUP):
            h = j * Q_GROUP + g
            st = scores[h]
            m_prev = m_scr[h]
            m_new = jnp.maximum(m_prev, jnp.max(st, axis=0, keepdims=True))
            alpha = jnp.exp(m_prev - m_new)
            p = jnp.exp(st - m_new)
            l_scr[h] = alpha * l_scr[h] + jnp.sum(p, axis=0, keepdims=True)
            acc_scr[h] = alpha * acc_scr[h] + _dot(vt_ref[j], p.astype(BF16))
            m_scr[h] = m_new

    @pl.when(kc == pl.num_programs(2) - 1)
    def _():
        lane = lax.broadcasted_iota(jnp.int32, (tq, LANES), 1)
        for c in range(KV_HEADS * Q_GROUP // 2):
            even = (acc_scr[2 * c] / l_scr[2 * c]).T
            odd = (acc_scr[2 * c + 1] / l_scr[2 * c + 1]).T
            o_ref[:, c * LANES:(c + 1) * LANES] = jnp.where(lane < HEAD_DIM, even, odd).astype(BF16)


def _att_dense(qt, k_all, vt_all, *, nctx, dec_batch, dec_seq):
    tq = ATT_B_TQ
    nblk = dec_seq // tq
    base = nctx // tq
    nkeys = k_all.shape[2]
    tk = next(t for t in ATT_B_TK if nkeys % t == 0)
    nheads = KV_HEADS * Q_GROUP
    return pl.pallas_call(
        _att_dense_kernel,
        grid=(dec_batch, nblk, nkeys // tk),
        in_specs=[pl.BlockSpec((KV_HEADS, Q_GROUP, LANES, tq),
                               lambda b, i, k: (0, 0, 0, base + b * nblk + i)),
                  pl.BlockSpec((None, KV_HEADS, tk, LANES), lambda b, i, k: (b, 0, k, 0)),
                  pl.BlockSpec((None, KV_HEADS, LANES, tk), lambda b, i, k: (b, 0, 0, k))],
        out_specs=pl.BlockSpec((tq, Q_WIDTH), lambda b, i, k: (b * nblk + i, 0)),
        out_shape=jax.ShapeDtypeStruct((dec_batch * dec_seq, Q_WIDTH), BF16),
        scratch_shapes=[pltpu.VMEM((nheads, 1, tq), F32), pltpu.VMEM((nheads, 1, tq), F32),
                        pltpu.VMEM((nheads, LANES, tq), F32)],
        compiler_params=_cparams(("parallel", "parallel", "arbitrary")),
        name="att_dense",
    )(qt, k_all, vt_all)


def _lru_kernel(xf_ref, xfp_ref, xfn_ref, xb_ref, xbp_ref, xbn_ref, cw_ref, cb_ref,
                wa_hi_ref, wa_lo_ref, wx_hi_ref, wx_lo_ref, ba_ref, bx_ref, lam_ref,
                h0f_ref, h0b_ref, hf_ref, hb_ref,
                af_scr, bf_scr, ab_scr, bb_scr, sf_scr, sb_scr, *, nctx_blk, seq_blk):
    i = pl.program_id(0)
    nchunk = pl.num_programs(0)
    ts = xf_ref.shape[0]
    row = lax.broadcasted_iota(jnp.int32, (ts, LRU_WIDTH), 0)
    tile_row = lax.broadcasted_iota(jnp.int32, (SUBLANES, LRU_WIDTH), 0)

    def chunk_info(c):
        lat = c >= nctx_blk
        pos = (c - nctx_blk) % seq_blk
        starts = jnp.logical_or(jnp.logical_not(lat), pos == 0)
        ends = jnp.logical_or(jnp.logical_not(lat), pos == seq_blk - 1)
        group = jnp.where(lat, 1 + (c - nctx_blk) // seq_blk, 0)
        return starts, ends, group

    def gates(x_ref, xp_ref, xn_ref, starts, ends, d, a_scr, b_scr):
        x = x_ref[...]
        prev = jnp.where(starts, 0.0, xp_ref[SUBLANES - 1:SUBLANES, :])
        nxt0 = jnp.where(ends, 0.0, xn_ref[0:1, :])
        nxt1 = jnp.where(ends, 0.0, xn_ref[1:2, :])
        x_m1 = jnp.where(row == 0, prev, pltpu.roll(x, 1, 0))
        x_p1 = jnp.where(row == ts - 1, nxt0, pltpu.roll(x, ts - 1, 0))
        x_p2 = jnp.where(row == ts - 2, nxt0, jnp.where(row == ts - 1, nxt1, pltpu.roll(x, ts - 2, 0)))
        xc = (cw_ref[0:1, :] * x_m1 + cw_ref[1:2, :] * x + cw_ref[2:3, :] * x_p1
              + cw_ref[3:4, :] * x_p2 + cb_ref[...])
        xc_hi, xc_lo = _split(xc)

        def blockdiag(hi_ref, lo_ref):
            return _dot(xc_hi, hi_ref[d]) + _dot(xc_lo, hi_ref[d]) + _dot(xc_hi, lo_ref[d])

        r = _sigmoid(blockdiag(wa_hi_ref, wa_lo_ref) + ba_ref[d])
        g = _sigmoid(blockdiag(wx_hi_ref, wx_lo_ref) + bx_ref[d])
        neg_lam = -lam_ref[d]
        softplus = jnp.maximum(neg_lam, 0.0) + jnp.log1p(jnp.exp(-jnp.abs(neg_lam)))
        log_a = -LRU_C * r * softplus
        a = jnp.exp(log_a)
        a_scr[...] = a
        b_scr[...] = jnp.sqrt(1.0 - a * a) * (g * xc)

    f_starts, f_ends, f_group = chunk_info(i)
    cb = nchunk - 1 - i
    b_starts, b_ends, b_group = chunk_info(cb)
    gates(xf_ref, xfp_ref, xfn_ref, f_starts, f_ends, 0, af_scr, bf_scr)
    gates(xb_ref, xbp_ref, xbn_ref, b_starts, b_ends, 1, ab_scr, bb_scr)

    @pl.when(f_starts)
    def _():
        sf_scr[...] = jnp.broadcast_to(h0f_ref[pl.ds(f_group, 1), :], sf_scr.shape)

    @pl.when(b_ends)
    def _():
        sb_scr[...] = jnp.broadcast_to(h0b_ref[pl.ds(b_group, 1), :], sb_scr.shape)

    ntile = ts // SUBLANES

    def tile_step(k, carry):
        hf, hb = carry
        base_f = pl.multiple_of(k * SUBLANES, SUBLANES)
        base_b = pl.multiple_of((ntile - 1 - k) * SUBLANES, SUBLANES)
        a_f = af_scr[pl.ds(base_f, SUBLANES), :]
        b_f = bf_scr[pl.ds(base_f, SUBLANES), :]
        a_b = ab_scr[pl.ds(base_b, SUBLANES), :]
        b_b = bb_scr[pl.ds(base_b, SUBLANES), :]
        out_f = jnp.zeros((SUBLANES, LRU_WIDTH), F32)
        out_b = jnp.zeros((SUBLANES, LRU_WIDTH), F32)
        for r in range(SUBLANES):
            rb = SUBLANES - 1 - r
            hf = a_f[r:r + 1, :] * hf + b_f[r:r + 1, :]
            hb = a_b[rb:rb + 1, :] * hb + b_b[rb:rb + 1, :]
            out_f = jnp.where(tile_row == r, hf, out_f)
            out_b = jnp.where(tile_row == rb, hb, out_b)
        hf_ref[pl.ds(base_f, SUBLANES), :] = out_f
        hb_ref[pl.ds(base_b, SUBLANES), :] = out_b
        return hf, hb

    hf, hb = lax.fori_loop(0, ntile, tile_step, (sf_scr[0:1, :], sb_scr[0:1, :]))
    sf_scr[...] = jnp.broadcast_to(hf, sf_scr.shape)
    sb_scr[...] = jnp.broadcast_to(hb, sb_scr.shape)


def _lru(z1, conv_w, conv_b, wa_hi, wa_lo, wx_hi, wx_lo, ba, bx, lam, h0f, h0b, *, nctx, dec_seq):
    n = z1.shape[0]
    ts = TOKEN_TILE
    nchunk = n // ts
    xcol = COL_XR // LRU_WIDTH
    tiles = ts // SUBLANES
    last_tile = n // SUBLANES - 1

    def cur(rev):
        return lambda i: ((nchunk - 1 - i) if rev else i, xcol)

    def prev(rev):
        return lambda i: (jnp.maximum(((nchunk - 1 - i) if rev else i) * tiles - 1, 0), xcol)

    def nxt(rev):
        return lambda i: (jnp.minimum((((nchunk - 1 - i) if rev else i) + 1) * tiles, last_tile), xcol)

    x_specs = []
    for rev in (False, True):
        x_specs += [pl.BlockSpec((ts, LRU_WIDTH), cur(rev)),
                    pl.BlockSpec((SUBLANES, LRU_WIDTH), prev(rev)),
                    pl.BlockSpec((SUBLANES, LRU_WIDTH), nxt(rev))]
    full = lambda shape: pl.BlockSpec(shape, lambda i: (0,) * len(shape))
    w_spec = full((2, LRU_WIDTH, LRU_WIDTH))
    v_spec = full((2, 1, LRU_WIDTH))
    out_shape = jax.ShapeDtypeStruct((n, LRU_WIDTH), F32)
    scr = pltpu.VMEM((ts, LRU_WIDTH), F32)
    state = pltpu.VMEM((SUBLANES, LRU_WIDTH), F32)
    return pl.pallas_call(
        functools.partial(_lru_kernel, nctx_blk=nctx // ts, seq_blk=dec_seq // ts),
        grid=(nchunk,),
        in_specs=x_specs + [full((4, LRU_WIDTH)), full((1, LRU_WIDTH)),
                            w_spec, w_spec, w_spec, w_spec, v_spec, v_spec, v_spec,
                            full((SUBLANES, LRU_WIDTH)), full((SUBLANES, LRU_WIDTH))],
        out_specs=[pl.BlockSpec((ts, LRU_WIDTH), lambda i: (i, 0)),
                   pl.BlockSpec((ts, LRU_WIDTH), lambda i: (nchunk - 1 - i, 0))],
        out_shape=[out_shape, out_shape],
        scratch_shapes=[scr, scr, scr, scr, state, state],
        compiler_params=_cparams(("arbitrary",)),
        name="lru",
    )(z1, z1, z1, z1, z1, z1, conv_w, conv_b, wa_hi, wa_lo, wx_hi, wx_lo, ba, bx, lam, h0f, h0b)


def _merge_kernel(x_ref, mod_ref, oa_ref, ob_ref, hf_ref, hb_ref, z2_ref,
                  woa_ref, wob_ref, woc_ref, wout_ref, o_ref):
    d = x_ref.shape[1]
    oc = (hf_ref[...] + hb_ref[...]) * _gelu(z2_ref[:, 0:LRU_WIDTH])
    ga = z2_ref[:, LRU_WIDTH:LRU_WIDTH + d]
    gb = z2_ref[:, LRU_WIDTH + d:LRU_WIDTH + 2 * d]
    gc = z2_ref[:, LRU_WIDTH + 2 * d:LRU_WIDTH + 3 * d]
    merged = (_sigmoid(ga) * _dot(oa_ref[...], woa_ref[...])
              + _sigmoid(gb) * _dot(ob_ref[...], wob_ref[...])
              + _sigmoid(gc) * _dot(oc.astype(BF16), woc_ref[...]))
    out = _dot(merged.astype(BF16), wout_ref[...])
    gate = mod_ref[0][:, 2 * d:3 * d]
    o_ref[...] = x_ref[...] + gate * out


def _merge(x, mod3, oa, ob, hf, hb, z2, wo_a, wo_b, wo_c, w_out, *, nctx, dec_seq):
    n, d = x.shape
    tm = TOKEN_TILE
    row = lambda w: pl.BlockSpec((tm, w), lambda i: (i, 0))
    full = lambda a: pl.BlockSpec(a.shape, lambda i: (0, 0))
    return pl.pallas_call(
        _merge_kernel,
        grid=(n // tm,),
        in_specs=[row(d),
                  pl.BlockSpec((1, 1, mod3.shape[2]),
                               lambda i: (_group_of_block(i, tm, nctx, dec_seq), 0, 0)),
                  row(Q_WIDTH), row(Q_WIDTH), row(LRU_WIDTH), row(LRU_WIDTH), row(Z2_WIDTH),
                  full(wo_a), full(wo_b), full(wo_c), full(w_out)],
        out_specs=row(d),
        out_shape=jax.ShapeDtypeStruct((n, d), F32),
        compiler_params=_cparams(("parallel",)),
        name="merge",
    )(x, mod3, oa, ob, hf, hb, z2, wo_a, wo_b, wo_c, w_out)


ORDER_SENTINEL = 1e9


def _candidate_groups():
    k, s = PEER_TOPK, SUBLANES
    split = 4
    groups, covered = [], set()
    for a in range(split):
        for b0 in range(0, k // (a + 1), s):
            groups.append((True, a, b0, 0))
            covered |= {(a, b) for b in range(b0, b0 + s)}
    for b in range(k // (split + 1)):
        for a0 in range(0, k // (b + 1), s):
            if a0 + s > split:
                groups.append((False, a0, b, max(split, a0)))
                covered |= {(a, b) for a in range(max(split, a0), a0 + s)}
    needed = {(a, b) for a in range(k) for b in range(k) if (a + 1) * (b + 1) <= k}
    assert needed <= covered and len(covered) == sum(
        s if f else a0 + s - amin for f, a0, _, amin in groups)
    return groups


class _TopRows:
    def __init__(self, s, extras=(), order=None):
        self.s = s
        self.rows = lax.broadcasted_iota(jnp.int32, s.shape, 0).astype(F32) if order is None else order
        self.extras = extras
        self.vals, self.idxs, self.picked = [], [], [[] for _ in extras]

    def round(self):
        m = jnp.max(self.s, axis=0, keepdims=True)
        pos = jnp.min(jnp.where(self.s == m, self.rows, ORDER_SENTINEL), axis=0, keepdims=True)
        sel = self.rows == pos
        self.vals.append(m)
        self.idxs.append(pos)
        for dst, ex in zip(self.picked, self.extras):
            dst.append(jnp.max(jnp.where(sel, ex, -1.0), axis=0, keepdims=True))
        self.s = jnp.where(sel, -jnp.inf, self.s)

    def result(self):
        cat = lambda xs: jnp.concatenate(xs, axis=0)
        return cat(self.vals), cat(self.idxs), [cat(p) for p in self.picked]


def _peer_select_steps(score_tiles, out):
    subs = [[_TopRows(s) for s in scores] for scores in score_tiles]
    for _ in range(PEER_TOPK):
        for pair in subs:
            for sel in pair:
                sel.round()
        yield
    cands = [_peer_candidates(*[sel.result()[:2] for sel in pair]) for pair in subs]
    for _ in range(PEER_TOPK):
        for sel in cands:
            sel.round()
        yield
    for sel in cands:
        best, _, (e1, e2) = sel.result()
        ex = jnp.exp(best - best[0:1])
        out.append((ex / jnp.sum(ex, axis=0, keepdims=True), e1, e2))


def _peer_topk_kernel(q_ref, khi_ref, klo_ref, g_ref, i1_ref, i2_ref):
    for t0 in range(0, q_ref.shape[0], LANES):
        out = []
        for _ in _peer_select_steps([_peer_scores(q_ref, khi_ref, klo_ref, t0)], out):
            pass
        g_ref[0, :, t0:t0 + LANES], i1_ref[0, :, t0:t0 + LANES], i2_ref[0, :, t0:t0 + LANES] = out[0]


def _peer_scores(q_ref, khi_ref, klo_ref, t0):
    scores = []
    for p in range(2):
        q_hi, q_lo = _split(q_ref[t0:t0 + LANES, p * PEER_DKEY:(p + 1) * PEER_DKEY])
        k_hi, k_lo = khi_ref[0, p], klo_ref[0, p]
        scores.append(_dot_nt(k_hi, q_hi) + _dot_nt(k_lo, q_hi) + _dot_nt(k_hi, q_lo))
    return scores


def _peer_candidates(first, second):
    top_s, top_i = (first[0], second[0]), (first[1], second[1])
    tb = top_s[0].shape[1]
    gshape = (SUBLANES, tb)
    sub_row = lax.broadcasted_iota(jnp.int32, gshape, 0).astype(F32)
    cand_s, cand_1, cand_2, order = [], [], [], []
    for fixed_first, a, b, a_min in _candidate_groups():
        if fixed_first:
            bcast = lambda x: jnp.broadcast_to(x[a:a + 1], gshape)
            cand_s.append(bcast(top_s[0]) + top_s[1][b:b + SUBLANES])
            cand_1.append(bcast(top_i[0]))
            cand_2.append(top_i[1][b:b + SUBLANES])
            order.append(float(a * PEER_TOPK + b) + sub_row)
        else:
            bcast = lambda x: jnp.broadcast_to(x[b:b + 1], gshape)
            live = sub_row >= float(a_min - a)
            cand_s.append(jnp.where(live, top_s[0][a:a + SUBLANES] + bcast(top_s[1]), -jnp.inf))
            cand_1.append(top_i[0][a:a + SUBLANES])
            cand_2.append(bcast(top_i[1]))
            order.append(jnp.where(live, float(a * PEER_TOPK + b) + float(PEER_TOPK) * sub_row,
                                   ORDER_SENTINEL))
    cat = lambda xs: jnp.concatenate(xs, axis=0)
    return _TopRows(cat(cand_s), (cat(cand_1), cat(cand_2)), cat(order))


def _peer_topk(q, keys_hi, keys_lo):
    n = q.shape[0]
    tb = PEER_TB
    spec = pl.BlockSpec((1, PEER_TOPK, tb), lambda i, h: (h, 0, i))
    shape = jax.ShapeDtypeStruct((PEER_HEADS, PEER_TOPK, n), F32)
    k_spec = pl.BlockSpec((1, 2, PEER_NKEYS, PEER_DKEY), lambda i, h: (h, 0, 0, 0))
    return pl.pallas_call(
        _peer_topk_kernel,
        grid=(n // tb, PEER_HEADS),
        in_specs=[pl.BlockSpec((tb, 2 * PEER_DKEY), lambda i, h: (i, h)), k_spec, k_spec],
        out_specs=[spec, spec, spec],
        out_shape=[shape, shape, shape],
        compiler_params=_cparams(("parallel", "parallel")),
        name="peer_topk",
    )(q, keys_hi, keys_lo)


def _peer_expert_kernel(x_ref, mod_ref, gn_ref, first_g_ref, first_i1_ref, first_i2_ref,
                        qn_ref, khi_ref, klo_ref, ut_ref, v_ref, o_ref,
                        h_scr, w_scr, p_scr, acc_scr, pair_scr, tok_scr):
    i = pl.program_id(0)
    c = pl.program_id(1)
    tb, d = x_ref.shape
    ec = ut_ref.shape[1]
    sub = ec // PEER_NKEYS
    npair = PEER_HEADS * PEER_TOPK
    g_ref, i1_ref, i2_ref = tok_scr.at[0], tok_scr.at[1], tok_scr.at[2]

    @pl.when(jnp.logical_and(c == 0, i == 0))
    def _():
        for k, ref in enumerate((first_g_ref, first_i1_ref, first_i2_ref)):
            tok_scr[k] = ref[...].reshape(npair, tb).T

    @pl.when(jnp.logical_and(c == 0, i > 0))
    def _():
        for k in range(3):
            tok_scr[k] = pair_scr[k].T

    @pl.when(c == 0)
    def _():
        x = x_ref[...]
        y = x * lax.rsqrt(jnp.mean(x * x, axis=-1, keepdims=True) + EPS) * gn_ref[...]
        m = mod_ref[0]
        h_scr[...] = (y * (1.0 + m[:, 4 * d:5 * d]) + m[:, 3 * d:4 * d]).astype(BF16)
        acc_scr[...] = jnp.zeros(acc_scr.shape, F32)
        key_row = lax.broadcasted_iota(jnp.int32, (PEER_NKEYS, npair), 0).astype(F32)

        def token_group(tg, carry):
            for u in range(PEER_TOKEN_UNROLL):
                t = tg * PEER_TOKEN_UNROLL + u
                gate = g_ref[pl.ds(t, 1), :]
                ga = jnp.where(key_row == i1_ref[pl.ds(t, 1), :], gate, 0.0).astype(BF16)
                ob = jnp.where(key_row == i2_ref[pl.ds(t, 1), :], 1.0, 0.0).astype(BF16)
                w_scr[pl.ds(pl.multiple_of(t * PEER_W_PITCH, SUBLANES), PEER_NKEYS), :] = _dot_nt(ga, ob)
            return carry

        lax.fori_loop(0, tb // PEER_TOKEN_UNROLL, token_group, 0)

    next_scores = [_peer_scores(qn_ref, khi_ref, klo_ref, t0) for t0 in range(0, tb, LANES)]

    selected = []
    steps = _peer_select_steps(next_scores, selected)

    def advance(n):
        for _ in range(n):
            next(steps, None)

    h1 = _dot(h_scr[...], ut_ref[...])
    advance(PEER_TOPK)
    ngroup = 4
    per_group = sub // ngroup
    acc = None
    for grp in range(ngroup):
        lo = grp * per_group * PEER_NKEYS
        hi = lo + per_group * PEER_NKEYS
        for s in range(grp * per_group, (grp + 1) * per_group):
            w = w_scr[pl.ds(c * sub + s, tb, stride=PEER_W_PITCH), :]
            p_scr[:, s * PEER_NKEYS:(s + 1) * PEER_NKEYS] = (
                w * _gelu(h1[:, s * PEER_NKEYS:(s + 1) * PEER_NKEYS])).astype(BF16)
        part = _dot(p_scr[:, lo:hi], v_ref[lo:hi, :])
        acc = part if acc is None else acc + part
        advance(PEER_TOPK // ngroup)
    acc_scr[...] += acc
    advance(1)

    head_rows = pl.ds(pl.multiple_of(c * PEER_TOPK, PEER_TOPK), PEER_TOPK)
    for t0, (g, e1, e2) in zip(range(0, tb, LANES), selected):
        pair_scr[0, head_rows, t0:t0 + LANES] = g
        pair_scr[1, head_rows, t0:t0 + LANES] = e1
        pair_scr[2, head_rows, t0:t0 + LANES] = e2

    @pl.when(c == pl.num_programs(1) - 1)
    def _():
        o_ref[...] = x_ref[...] + mod_ref[0][:, 5 * d:6 * d] * acc_scr[...]


def _peer_expert(x, mod3, gain, q, keys_hi, keys_lo, ut, v, *, nctx, dec_seq):
    n, d = x.shape
    tb, ec = PEER_TB, PEER_EC
    nexp = v.shape[0]
    ntile = n // tb
    npair = PEER_HEADS * PEER_TOPK
    assert nexp // ec == PEER_HEADS
    first = _peer_topk(q[:tb], keys_hi, keys_lo)
    row = lambda w: pl.BlockSpec((tb, w), lambda i, c: (i, 0))
    first_spec = pl.BlockSpec((PEER_HEADS, PEER_TOPK, tb), lambda i, c: (0, 0, 0))
    k_spec = pl.BlockSpec((1, 2, PEER_NKEYS, PEER_DKEY), lambda i, c: (c, 0, 0, 0))
    return pl.pallas_call(
        _peer_expert_kernel,
        grid=(ntile, nexp // ec),
        in_specs=[row(d),
                  pl.BlockSpec((1, 1, mod3.shape[2]),
                               lambda i, c: (_group_of_block(i, tb, nctx, dec_seq), 0, 0)),
                  pl.BlockSpec((1, d), lambda i, c: (0, 0)),
                  first_spec, first_spec, first_spec,
                  pl.BlockSpec((tb, 2 * PEER_DKEY), lambda i, c: (jnp.minimum(i + 1, ntile - 1), c)),
                  k_spec, k_spec,
                  pl.BlockSpec((d, ec), lambda i, c: (0, c)),
                  pl.BlockSpec((ec, d), lambda i, c: (c, 0))],
        out_specs=row(d),
        out_shape=jax.ShapeDtypeStruct((n, d), F32),
        scratch_shapes=[pltpu.VMEM((tb, d), BF16),
                        pltpu.VMEM((tb * PEER_W_PITCH, PEER_NKEYS), F32),
                        pltpu.VMEM((tb, ec), BF16),
                        pltpu.VMEM((tb, d), F32),
                        pltpu.VMEM((3, npair, tb), F32),
                        pltpu.VMEM((3, tb, npair), F32)],
        compiler_params=_cparams(("arbitrary", "arbitrary")),
        name="peer_expert",
    )(x, mod3, gain.reshape(1, d), *first, q, keys_hi, keys_lo, ut, v)


def _final_norm_kernel(x_ref, g_ref, o_ref):
    x = x_ref[...]
    o_ref[...] = x * lax.rsqrt(jnp.mean(x * x, axis=-1, keepdims=True) + EPS) * g_ref[...]


def _final_norm(x, gain):
    n, d = x.shape
    tm = 512
    return pl.pallas_call(
        _final_norm_kernel,
        grid=(n // tm,),
        in_specs=[pl.BlockSpec((tm, d), lambda i: (i, 0)), pl.BlockSpec((1, d), lambda i: (0, 0))],
        out_specs=pl.BlockSpec((tm, d), lambda i: (i, 0)),
        out_shape=jax.ShapeDtypeStruct((n, d), F32),
        compiler_params=_cparams(("parallel",)),
        name="final_norm",
    )(x, gain.reshape(1, d))


def _rope_tables(dec_seq, tile):
    t = jnp.arange(dec_seq)
    n_freq = HEAD_DIM // 4
    inv = ROPE_BASE ** (-jnp.arange(n_freq, dtype=F32) / n_freq)
    ang = jnp.concatenate([(t // GRID_W).astype(F32)[:, None] * inv,
                           (t % GRID_W).astype(F32)[:, None] * inv], axis=-1)
    cos, sin = jnp.cos(ang), jnp.sin(ang)
    reps = LANES // HEAD_DIM
    cos_t = jnp.tile(jnp.concatenate([cos, cos], axis=-1), (1, reps))
    sin_t = jnp.tile(jnp.concatenate([-sin, sin], axis=-1), (1, reps))
    cos_t = jnp.concatenate([jnp.ones((tile, LANES), F32), cos_t], axis=0)
    sin_t = jnp.concatenate([jnp.zeros((tile, LANES), F32), sin_t], axis=0)
    return cos_t, sin_t


def _dup_cache(cache):
    c = jnp.transpose(cache, (0, 2, 1, 3))
    return jnp.concatenate([c, c], axis=-1).astype(BF16)


def _dense_blockdiag(w):
    dirs, nb, bw, _ = w.shape
    eye = jnp.eye(nb, dtype=w.dtype)
    return jnp.einsum('dncf,nm->dncmf', w, eye).reshape(dirs, nb * bw, nb * bw)


def _pad_rows(a, rows):
    return jnp.concatenate([a, jnp.zeros((rows - a.shape[0],) + a.shape[1:], a.dtype)], axis=0)


def kernel(x_prompt, x_sample, c, cache_wa_k, cache_wa_v, cache_ax_k, cache_ax_v, state_lru_fwd,
           state_lru_bwd, c_ctx, w_ada, b_ada, g_norm1, w_in, wa_sink, ax_q_gain, ax_k_gain, conv_w,
           conv_b, lru_wa, lru_ba, lru_wx, lru_bx, lru_lambda, wo_a, wo_b, wo_c, w_out, g_norm2,
           peer_wq, peer_keys, peer_u, peer_v, g_final):
    batch, seq, d = x_prompt.shape
    dec_batch, dec_seq, _ = x_sample.shape
    depth = w_in.shape[0]
    nctx = batch * seq
    nlat = dec_batch * dec_seq
    assert seq == TOKEN_TILE and dec_seq % TOKEN_TILE == 0 and 1 + dec_batch <= SUBLANES
    assert w_in.shape[2] == Z1_WIDTH + Z2_WIDTH and d == 1024
    sizes = dict(nctx=nctx, dec_seq=dec_seq)

    x = jnp.concatenate([x_prompt.reshape(nctx, d), x_sample.reshape(nlat, d)], axis=0)
    cvec = _pad_rows(jnp.concatenate([c_ctx[None, :], c], axis=0), SUBLANES)
    mods = _ada(cvec, w_ada, b_ada)

    cos_tab, sin_tab = _rope_tables(dec_seq, TOKEN_TILE)
    head_mean = jnp.kron(jnp.eye(LANES // HEAD_DIM, dtype=F32),
                         jnp.full((HEAD_DIM, HEAD_DIM), 1.0 / HEAD_DIM, F32)).astype(BF16)
    reps = LANES // HEAD_DIM

    new_wa_k, new_wa_v, new_ax_k, new_ax_v, new_hf, new_hb = [], [], [], [], [], []
    for l in range(depth):
        mod3 = mods[l].reshape(SUBLANES, 1, 6 * d)
        w_in_bf = w_in[l].astype(BF16)
        z1 = _modmm(x, mod3, g_norm1[l], w_in_bf[:, :Z1_WIDTH], None, shift_col=0, tn=Z1_WIDTH, **sizes)
        z2 = _modmm(x, mod3, g_norm1[l], w_in_bf[:, Z1_WIDTH:], None, shift_col=0, tn=Z2_WIDTH, **sizes)

        gq = jnp.tile(ax_q_gain[l], reps).reshape(1, LANES)
        gk = jnp.tile(ax_k_gain[l], reps).reshape(1, LANES)
        qsa, kka, vva, qsb, kkb, vvb, kbn, qtb = _prep(z1, cos_tab, sin_tab, gq, gk, head_mean, **sizes)

        oa_ctx, ob_ctx = _att_ctx(wa_sink[l], qsa, kka, vva, qsb, kkb, vvb, nseq=batch, seq=seq)
        oa_lat = _att_win(wa_sink[l], qsa, kka, vva, _dup_cache(cache_wa_k[:, l]),
                          _dup_cache(cache_wa_v[:, l]), nctx=nctx, dec_batch=dec_batch, dec_seq=dec_seq)

        def with_cache(cache, cur):
            lat = cur[:, nctx:].reshape(KV_HEADS, dec_batch, dec_seq, LANES)
            return jnp.concatenate([_dup_cache(cache), jnp.transpose(lat, (1, 0, 2, 3))], axis=2)

        ob_lat = _att_dense(qtb, with_cache(cache_ax_k[:, l], kkb),
                            jnp.swapaxes(with_cache(cache_ax_v[:, l], vvb), 2, 3),
                            nctx=nctx, dec_batch=dec_batch, dec_seq=dec_seq)
        oa = jnp.concatenate([oa_ctx, oa_lat], axis=0)
        ob = jnp.concatenate([ob_ctx, ob_lat], axis=0)

        wa_hi, wa_lo = _split(_dense_blockdiag(lru_wa[l]))
        wx_hi, wx_lo = _split(_dense_blockdiag(lru_wx[l]))
        vec = lambda a: a.reshape(2, 1, LRU_WIDTH)
        h0f = _pad_rows(jnp.concatenate([jnp.zeros((1, LRU_WIDTH), F32), state_lru_fwd[:, l]], axis=0), SUBLANES)
        h0b = _pad_rows(jnp.concatenate([jnp.zeros((1, LRU_WIDTH), F32), state_lru_bwd[:, l]], axis=0), SUBLANES)
        hf, hb = _lru(z1, conv_w[l], conv_b[l].reshape(1, LRU_WIDTH), wa_hi, wa_lo, wx_hi, wx_lo,
                      vec(lru_ba[l]), vec(lru_bx[l]), vec(lru_lambda[l]), h0f, h0b, **sizes)

        x = _merge(x, mod3, oa, ob, hf, hb, z2, wo_a[l].astype(BF16), wo_b[l].astype(BF16),
                   wo_c[l].astype(BF16), w_out[l].astype(BF16), **sizes)

        wq_hi, wq_lo = _split(peer_wq[l])
        q = _modmm(x, mod3, g_norm2[l], wq_hi, wq_lo, shift_col=3 * d, tn=wq_hi.shape[1], **sizes)
        keys_hi, keys_lo = _split(peer_keys[l])
        x = _peer_expert(x, mod3, g_norm2[l], q, keys_hi, keys_lo,
                         peer_u[l].T.astype(BF16), peer_v[l].astype(BF16), **sizes)

        ctx4 = lambda a: a[:nctx].reshape(batch, seq, KV_HEADS, HEAD_DIM)
        new_wa_k.append(ctx4(z1[:, COL_KA:COL_KA + KV_WIDTH]))
        new_wa_v.append(ctx4(z1[:, COL_VA:COL_VA + KV_WIDTH]))
        new_ax_k.append(ctx4(kbn))
        new_ax_v.append(ctx4(z1[:, COL_VB:COL_VB + KV_WIDTH]))
        new_hf.append(hf[:nctx].reshape(batch, seq, LRU_WIDTH)[:, -1])
        new_hb.append(hb[:nctx].reshape(batch, seq, LRU_WIDTH)[:, 0])

    y = _final_norm(x, g_final)
    stack = lambda xs: jnp.stack(xs, axis=1)
    return (y[:nctx].reshape(batch, seq, d), y[nctx:].reshape(dec_batch, dec_seq, d),
            stack(new_wa_k), stack(new_wa_v), stack(new_ax_k), stack(new_ax_v),
            stack(new_hf), stack(new_hb))
```

```python
import functools

import jax
import jax.numpy as jnp
from jax import lax
from jax.experimental import pallas as pl
from jax.experimental.pallas import tpu as pltpu

F32 = jnp.float32
BF16 = jnp.bfloat16

HEAD_DIM = 64
KV_HEADS = 2
Q_GROUP = 4
Q_WIDTH = KV_HEADS * Q_GROUP * HEAD_DIM
KV_WIDTH = KV_HEADS * HEAD_DIM
WINDOW = 128
GRID_W = 64
LRU_WIDTH = 512
LRU_BLOCKS = 8
LRU_C = 8.0
PEER_HEADS = 8
PEER_NKEYS = 128
PEER_DKEY = 128
PEER_TOPK = 16
ROPE_BASE = 10000.0
EPS = 1e-6
NEG = -1e30
LANES = 128
SUBLANES = 8
VMEM_LIMIT = 56 * 1024 * 1024

Z1_WIDTH = 2048
Z2_WIDTH = 3584
COL_KA, COL_VA, COL_QB, COL_KB, COL_VB, COL_XR = 512, 640, 768, 1280, 1408, 1536

TOKEN_TILE = 256
ATT_A_TQ = 128
ATT_B_TQ = 256
ATT_B_TK = (1536, 1024, 512)
PEER_TB = 256
PEER_EC = 2048
PEER_BLOCK = 256
PEER_TOKEN_UNROLL = 64
PEER_W_PITCH = PEER_NKEYS + SUBLANES


def _cparams(sem):
    return pltpu.CompilerParams(dimension_semantics=sem, vmem_limit_bytes=VMEM_LIMIT)


def _split(x):
    hi = x.astype(BF16)
    lo = (x - hi.astype(F32)).astype(BF16)
    return hi, lo


def _dot(a, b):
    return jnp.dot(a, b, preferred_element_type=F32)


def _dot_nt(a, b):
    return lax.dot_general(a, b, (((1,), (1,)), ((), ())), preferred_element_type=F32)


def _sigmoid(x):
    return 1.0 / (1.0 + jnp.exp(-x))


def _gelu_inner(x):
    return 0.7978845608028654 * (x + 0.044715 * (x * x * x))


def _gelu(x):
    return 0.5 * x * (1.0 + jnp.tanh(_gelu_inner(x)))


def _group_of_block(i, tile, nctx, dec_seq):
    row = i * tile
    return jnp.where(row < nctx, 0, 1 + (row - nctx) // dec_seq)


def _ada_kernel(c_ref, w_ref, b_ref, o_ref):
    c = c_ref[...]
    s = c * _sigmoid(c)
    s_hi, s_lo = _split(s)
    w_hi, w_lo = _split(w_ref[0])
    o_ref[0] = _dot(s_hi, w_hi) + _dot(s_lo, w_hi) + _dot(s_hi, w_lo) + b_ref[0]


def _ada(cvec, w_ada, b_ada):
    depth, d, e = w_ada.shape
    tn = 1536
    return pl.pallas_call(
        _ada_kernel,
        grid=(depth, e // tn),
        in_specs=[
            pl.BlockSpec((SUBLANES, d), lambda l, j: (0, 0)),
            pl.BlockSpec((1, d, tn), lambda l, j: (l, 0, j)),
            pl.BlockSpec((1, 1, tn), lambda l, j: (l, 0, j)),
        ],
        out_specs=pl.BlockSpec((1, SUBLANES, tn), lambda l, j: (l, 0, j)),
        out_shape=jax.ShapeDtypeStruct((depth, SUBLANES, e), F32),
        compiler_params=_cparams(("parallel", "parallel")),
        name="ada",
    )(cvec, w_ada, b_ada.reshape(depth, 1, e))


def _modmm_kernel(x_ref, mod_ref, g_ref, w_ref, *rest, shift_col, three_pass):
    d = x_ref.shape[1]
    if three_pass:
        wlo_ref, o_ref, h_scr, hlo_scr = rest
    else:
        o_ref, h_scr = rest

    @pl.when(pl.program_id(1) == 0)
    def _():
        x = x_ref[...]
        y = x * lax.rsqrt(jnp.mean(x * x, axis=-1, keepdims=True) + EPS) * g_ref[...]
        m = mod_ref[0]
        h = y * (1.0 + m[:, shift_col + d:shift_col + 2 * d]) + m[:, shift_col:shift_col + d]
        hi = h.astype(BF16)
        h_scr[...] = hi
        if three_pass:
            hlo_scr[...] = (h - hi.astype(F32)).astype(BF16)

    acc = _dot(h_scr[...], w_ref[...])
    if three_pass:
        acc = acc + _dot(hlo_scr[...], w_ref[...]) + _dot(h_scr[...], wlo_ref[...])
    o_ref[...] = acc


def _modmm(x, mod3, gain, w, w_lo, *, shift_col, tn, nctx, dec_seq):
    n, d = x.shape
    width = w.shape[1]
    tm = 512
    three_pass = w_lo is not None
    grp = lambda i, j: (_group_of_block(i, tm, nctx, dec_seq), 0, 0)
    in_specs = [
        pl.BlockSpec((tm, d), lambda i, j: (i, 0)),
        pl.BlockSpec((1, 1, mod3.shape[2]), grp),
        pl.BlockSpec((1, d), lambda i, j: (0, 0)),
        pl.BlockSpec((d, tn), lambda i, j: (0, j)),
    ]
    args = [x, mod3, gain.reshape(1, d), w]
    scratch = [pltpu.VMEM((tm, d), BF16)]
    if three_pass:
        in_specs.append(pl.BlockSpec((d, tn), lambda i, j: (0, j)))
        args.append(w_lo)
        scratch.append(pltpu.VMEM((tm, d), BF16))
    return pl.pallas_call(
        functools.partial(_modmm_kernel, shift_col=shift_col, three_pass=three_pass),
        grid=(n // tm, width // tn),
        in_specs=in_specs,
        out_specs=pl.BlockSpec((tm, tn), lambda i, j: (i, j)),
        out_shape=jax.ShapeDtypeStruct((n, width), F32),
        scratch_shapes=scratch,
        compiler_params=_cparams(("parallel", "arbitrary")),
        name="modmm",
    )(*args)


def _prep_kernel(z_ref, cos_ref, sin_ref, gq_ref, gk_ref, m_ref,
                 qsa_ref, kka_ref, vva_ref, qsb_ref, kkb_ref, vvb_ref, kbn_ref, qtb_ref):
    tm = z_ref.shape[0]
    lane = lax.broadcasted_iota(jnp.int32, (tm, LANES), 1)
    low_head = lane < HEAD_DIM
    first_half = (lane & (HEAD_DIM // 2)) == 0
    cos = cos_ref[...]
    sin = sin_ref[...]
    mmat = m_ref[...]
    scale = HEAD_DIM ** -0.5

    def rope(x):
        back = pltpu.roll(x, HEAD_DIM // 2, 1)
        fwd = pltpu.roll(x, LANES - HEAD_DIM // 2, 1)
        return x * cos + jnp.where(first_half, fwd, back) * sin

    def headnorm(x, gain):
        sq_hi, sq_lo = _split(x * x)
        ms = _dot(sq_hi, mmat) + _dot(sq_lo, mmat)
        return x * lax.rsqrt(ms + EPS) * gain

    def store_q(ref, c, q, t_ref=None):
        zero = jnp.zeros_like(q)
        j, g0 = c // 2, 2 * (c % 2)
        for g, qm in ((g0, jnp.where(low_head, q, zero)), (g0 + 1, jnp.where(low_head, zero, q))):
            ref[j, g] = qm.astype(BF16)
            if t_ref is not None:
                t_ref[j, g] = qm.T.astype(BF16)

    def store_dup(ref, x):
        swapped = pltpu.roll(x, HEAD_DIM, 1)
        ref[0] = jnp.where(low_head, x, swapped).astype(BF16)
        ref[1] = jnp.where(low_head, swapped, x).astype(BF16)

    for c in range(Q_WIDTH // LANES):
        store_q(qsa_ref, c, rope(z_ref[:, c * LANES:(c + 1) * LANES]) * scale)
        qb = headnorm(z_ref[:, COL_QB + c * LANES:COL_QB + (c + 1) * LANES], gq_ref[...])
        store_q(qsb_ref, c, rope(qb) * scale, qtb_ref)
    store_dup(kka_ref, rope(z_ref[:, COL_KA:COL_KA + KV_WIDTH]))
    store_dup(vva_ref, z_ref[:, COL_VA:COL_VA + KV_WIDTH])
    kb = rope(headnorm(z_ref[:, COL_KB:COL_KB + KV_WIDTH], gk_ref[...]))
    kbn_ref[...] = kb
    store_dup(kkb_ref, kb)
    store_dup(vvb_ref, z_ref[:, COL_VB:COL_VB + KV_WIDTH])


def _prep(z1, cos_tab, sin_tab, gq, gk, mmat, *, nctx, dec_seq):
    n = z1.shape[0]
    tm = TOKEN_TILE
    nctx_blk = nctx // tm
    seq_blk = dec_seq // tm

    def tab_idx(i):
        return (jnp.where(i < nctx_blk, 0, 1 + (i - nctx_blk) % seq_blk), 0)

    qs_spec = pl.BlockSpec((KV_HEADS, Q_GROUP, tm, LANES), lambda i: (0, 0, i, 0))
    kv_spec = pl.BlockSpec((KV_HEADS, tm, LANES), lambda i: (0, i, 0))
    qs_shape = jax.ShapeDtypeStruct((KV_HEADS, Q_GROUP, n, LANES), BF16)
    kv_shape = jax.ShapeDtypeStruct((KV_HEADS, n, LANES), BF16)
    return pl.pallas_call(
        _prep_kernel,
        grid=(n // tm,),
        in_specs=[
            pl.BlockSpec((tm, Z1_WIDTH), lambda i: (i, 0)),
            pl.BlockSpec((tm, LANES), tab_idx),
            pl.BlockSpec((tm, LANES), tab_idx),
            pl.BlockSpec((1, LANES), lambda i: (0, 0)),
            pl.BlockSpec((1, LANES), lambda i: (0, 0)),
            pl.BlockSpec((LANES, LANES), lambda i: (0, 0)),
        ],
        out_specs=[qs_spec, kv_spec, kv_spec, qs_spec, kv_spec, kv_spec,
                   pl.BlockSpec((tm, LANES), lambda i: (i, 0)),
                   pl.BlockSpec((KV_HEADS, Q_GROUP, LANES, tm), lambda i: (0, 0, 0, i))],
        out_shape=[qs_shape, kv_shape, kv_shape, qs_shape, kv_shape, kv_shape,
                   jax.ShapeDtypeStruct((n, LANES), F32),
                   jax.ShapeDtypeStruct((KV_HEADS, Q_GROUP, LANES, n), BF16)],
        compiler_params=_cparams(("parallel",)),
        name="prep",
    )(z1, cos_tab, sin_tab, gq, gk, mmat)


def _merge_heads(o, tq, c):
    lane = lax.broadcasted_iota(jnp.int32, (tq, LANES), 1)
    g = 2 * c
    return jnp.where(lane < HEAD_DIM, o[g * tq:(g + 1) * tq], o[(g + 1) * tq:(g + 2) * tq])


def _sink_column(sink_ref, j, tq):
    return jnp.concatenate(
        [jnp.full((tq, 1), sink_ref[j * Q_GROUP + g], F32) for g in range(Q_GROUP)], axis=0)


def _att_ctx_kernel(sink_ref, qsa_ref, kka_ref, vva_ref, qsb_ref, kkb_ref, vvb_ref, oa_ref, ob_ref):
    tq = qsa_ref.shape[2]
    mixers = ((qsa_ref, kka_ref, vva_ref, oa_ref, True), (qsb_ref, kkb_ref, vvb_ref, ob_ref, False))
    scores = [[_dot_nt(q_ref[j].reshape(Q_GROUP * tq, LANES), k_ref[j]) for j in range(KV_HEADS)]
              for q_ref, k_ref, _, _, _ in mixers]
    for (_, _, v_ref, o_ref, has_sink), mixer_scores in zip(mixers, scores):
        for j in range(KV_HEADS):
            s = mixer_scores[j]
            m = jnp.max(s, axis=-1, keepdims=True)
            if has_sink:
                sk = _sink_column(sink_ref, j, tq)
                m = jnp.maximum(m, sk)
            e = jnp.exp(s - m)
            den = jnp.sum(e, axis=-1, keepdims=True)
            if has_sink:
                den = den + jnp.exp(sk - m)
            o = _dot(e.astype(BF16), v_ref[j]) / den
            for c in range(2):
                col = (2 * j + c) * LANES
                o_ref[:, col:col + LANES] = _merge_heads(o, tq, c).astype(BF16)


def _att_ctx(sink, qsa, kka, vva, qsb, kkb, vvb, *, nseq, seq):
    qs_spec = pl.BlockSpec((KV_HEADS, Q_GROUP, seq, LANES), lambda i: (0, 0, i, 0))
    kv_spec = pl.BlockSpec((KV_HEADS, seq, LANES), lambda i: (0, i, 0))
    o_spec = pl.BlockSpec((seq, Q_WIDTH), lambda i: (i, 0))
    o_shape = jax.ShapeDtypeStruct((nseq * seq, Q_WIDTH), BF16)
    return pl.pallas_call(
        _att_ctx_kernel,
        grid=(nseq,),
        in_specs=[pl.BlockSpec(memory_space=pltpu.SMEM),
                  qs_spec, kv_spec, kv_spec, qs_spec, kv_spec, kv_spec],
        out_specs=[o_spec, o_spec],
        out_shape=[o_shape, o_shape],
        compiler_params=_cparams(("parallel",)),
        name="att_ctx",
    )(sink, qsa, kka, vva, qsb, kkb, vvb)


def _att_win_kernel(sink_ref, qs_ref, kp_ref, kc_ref, kn_ref, vp_ref, vc_ref, vn_ref,
                    kx_ref, vx_ref, o_ref, *, nblk):
    i = pl.program_id(1)
    tq = qs_ref.shape[2]
    rows = Q_GROUP * tq
    a_idx = lax.broadcasted_iota(jnp.int32, (rows, tq), 0) & (tq - 1)
    j_idx = lax.broadcasted_iota(jnp.int32, (rows, tq), 1)
    valid_prev = (j_idx >= a_idx) & (i > 0)
    valid_next = (j_idx <= a_idx) & (i < nblk - 1)
    raw = []
    for j in range(KV_HEADS):
        qs = qs_ref[j].reshape(rows, LANES)
        raw.append([_dot_nt(qs, k_ref[j]) for k_ref in (kx_ref, kp_ref, kc_ref, kn_ref)])
    for j in range(KV_HEADS):
        s_x, s_p, s_c, s_n = raw[j]
        s_p = jnp.where(valid_prev, s_p, NEG)
        s_n = jnp.where(valid_next, s_n, NEG)
        sk = _sink_column(sink_ref, j, tq)
        m = jnp.maximum(jnp.max(s_x, axis=-1, keepdims=True), sk)
        for s in (s_p, s_c, s_n):
            m = jnp.maximum(m, jnp.max(s, axis=-1, keepdims=True))
        den = jnp.exp(sk - m)
        o = jnp.zeros((rows, LANES), F32)
        for s, v_ref in ((s_x, vx_ref), (s_p, vp_ref), (s_c, vc_ref), (s_n, vn_ref)):
            e = jnp.exp(s - m)
            den = den + jnp.sum(e, axis=-1, keepdims=True)
            o = o + _dot(e.astype(BF16), v_ref[j])
        o = o / den
        for c in range(2):
            col = (2 * j + c) * LANES
            o_ref[:, col:col + LANES] = _merge_heads(o, tq, c).astype(BF16)


def _att_win(sink, qs, kk, vv, kx, vx, *, nctx, dec_batch, dec_seq):
    tq = ATT_A_TQ
    assert tq == WINDOW
    nblk = dec_seq // tq
    base = nctx // tq
    past = kx.shape[2]

    def q_idx(b, i):
        return (0, 0, base + b * nblk + i, 0)

    def kv_idx(delta):
        return lambda b, i: (0, base + b * nblk + jnp.clip(i + delta, 0, nblk - 1), 0)

    kv_specs = [pl.BlockSpec((KV_HEADS, tq, LANES), kv_idx(d)) for d in (-1, 0, 1)]
    x_spec = pl.BlockSpec((None, KV_HEADS, past, LANES), lambda b, i: (b, 0, 0, 0))
    return pl.pallas_call(
        functools.partial(_att_win_kernel, nblk=nblk),
        grid=(dec_batch, nblk),
        in_specs=[pl.BlockSpec(memory_space=pltpu.SMEM),
                  pl.BlockSpec((KV_HEADS, Q_GROUP, tq, LANES), q_idx)] + kv_specs + kv_specs
                 + [x_spec, x_spec],
        out_specs=pl.BlockSpec((tq, Q_WIDTH), lambda b, i: (b * nblk + i, 0)),
        out_shape=jax.ShapeDtypeStruct((dec_batch * dec_seq, Q_WIDTH), BF16),
        compiler_params=_cparams(("parallel", "parallel")),
        name="att_win",
    )(sink, qs, kk, kk, kk, vv, vv, vv, kx, vx)


def _att_dense_kernel(qt_ref, k_ref, vt_ref, o_ref, m_scr, acc_scr):
    kc = pl.program_id(2)
    tq = qt_ref.shape[3]

    @pl.when(kc == 0)
    def _():
        m_scr[...] = jnp.full(m_scr.shape, -jnp.inf, F32)
        acc_scr[...] = jnp.zeros(acc_scr.shape, F32)

    scores = [_dot(k_ref[j], qt_ref[j, g]) for j in range(KV_HEADS) for g in range(Q_GROUP)]
    for j in range(KV_HEADS):
        for g in range(Q_GROUP):
            h = j * Q_GROUP + g
            st = scores[h]
            m_prev = m_scr[h]
            m_new = jnp.maximum(m_prev, jnp.max(st, axis=0, keepdims=True))
            alpha = jnp.exp(m_prev - m_new)
            p = jnp.exp(st - m_new)
            acc_scr[h] = alpha * acc_scr[h] + _dot(vt_ref[g % 2, j], p.astype(BF16))
            m_scr[h] = m_new

    @pl.when(kc == pl.num_programs(2) - 1)
    def _():
        lane = lax.broadcasted_iota(jnp.int32, (tq, LANES), 1)
        for c in range(KV_HEADS * Q_GROUP // 2):
            even, odd = acc_scr[2 * c], acc_scr[2 * c + 1]
            even = (even / even[HEAD_DIM:HEAD_DIM + 1]).T
            odd = (odd / odd[0:1]).T
            o_ref[:, c * LANES:(c + 1) * LANES] = jnp.where(lane < HEAD_DIM, even, odd).astype(BF16)


def _att_dense(qt, k_all, vt_all, *, nctx, dec_batch, dec_seq):
    tq = ATT_B_TQ
    nblk = dec_seq // tq
    base = nctx // tq
    nkeys = k_all.shape[2]
    tk = next(t for t in ATT_B_TK if nkeys % t == 0)
    nheads = KV_HEADS * Q_GROUP
    return pl.pallas_call(
        _att_dense_kernel,
        grid=(dec_batch, nblk, nkeys // tk),
        in_specs=[pl.BlockSpec((KV_HEADS, Q_GROUP, LANES, tq),
                               lambda b, i, k: (0, 0, 0, base + b * nblk + i)),
                  pl.BlockSpec((None, KV_HEADS, tk, LANES), lambda b, i, k: (b, 0, k, 0)),
                  pl.BlockSpec((None, 2, KV_HEADS, LANES, tk), lambda b, i, k: (b, 0, 0, 0, k))],
        out_specs=pl.BlockSpec((tq, Q_WIDTH), lambda b, i, k: (b * nblk + i, 0)),
        out_shape=jax.ShapeDtypeStruct((dec_batch * dec_seq, Q_WIDTH), BF16),
        scratch_shapes=[pltpu.VMEM((nheads, 1, tq), F32), pltpu.VMEM((nheads, LANES, tq), F32)],
        compiler_params=_cparams(("parallel", "parallel", "arbitrary")),
        name="att_dense",
    )(qt, k_all, vt_all)


def _lru_kernel(xf_ref, xfp_ref, xfn_ref, xb_ref, xbp_ref, xbn_ref, cw_ref, cb_ref,
                wa_hi_ref, wa_lo_ref, wx_hi_ref, wx_lo_ref, ba_ref, bx_ref, lam_ref,
                h0f_ref, h0b_ref, hf_ref, hb_ref,
                af_scr, bf_scr, ab_scr, bb_scr, sf_scr, sb_scr, *, nctx_blk, seq_blk):
    i = pl.program_id(0)
    nchunk = pl.num_programs(0)
    ts = xf_ref.shape[0]
    row = lax.broadcasted_iota(jnp.int32, (ts, LRU_WIDTH), 0)
    tile_row = lax.broadcasted_iota(jnp.int32, (SUBLANES, LRU_WIDTH), 0)

    def chunk_info(c):
        lat = c >= nctx_blk
        pos = (c - nctx_blk) % seq_blk
        starts = jnp.logical_or(jnp.logical_not(lat), pos == 0)
        ends = jnp.logical_or(jnp.logical_not(lat), pos == seq_blk - 1)
        group = jnp.where(lat, 1 + (c - nctx_blk) // seq_blk, 0)
        return starts, ends, group

    def gates(x_ref, xp_ref, xn_ref, starts, ends, d, a_scr, b_scr):
        x = x_ref[...]
        prev = jnp.where(starts, 0.0, xp_ref[SUBLANES - 1:SUBLANES, :])
        nxt0 = jnp.where(ends, 0.0, xn_ref[0:1, :])
        nxt1 = jnp.where(ends, 0.0, xn_ref[1:2, :])
        x_m1 = jnp.where(row == 0, prev, pltpu.roll(x, 1, 0))
        x_p1 = jnp.where(row == ts - 1, nxt0, pltpu.roll(x, ts - 1, 0))
        x_p2 = jnp.where(row == ts - 2, nxt0, jnp.where(row == ts - 1, nxt1, pltpu.roll(x, ts - 2, 0)))
        xc = (cw_ref[0:1, :] * x_m1 + cw_ref[1:2, :] * x + cw_ref[2:3, :] * x_p1
              + cw_ref[3:4, :] * x_p2 + cb_ref[...])
        xc_hi, xc_lo = _split(xc)

        def blockdiag(hi_ref, lo_ref):
            return _dot(xc_hi, hi_ref[d]) + _dot(xc_lo, hi_ref[d]) + _dot(xc_hi, lo_ref[d])

        r = _sigmoid(blockdiag(wa_hi_ref, wa_lo_ref) + ba_ref[d])
        g = _sigmoid(blockdiag(wx_hi_ref, wx_lo_ref) + bx_ref[d])
        neg_lam = -lam_ref[d]
        softplus = jnp.maximum(neg_lam, 0.0) + jnp.log1p(jnp.exp(-jnp.abs(neg_lam)))
        log_a = -LRU_C * r * softplus
        a = jnp.exp(log_a)
        a_scr[...] = a
        b_scr[...] = jnp.sqrt(1.0 - a * a) * (g * xc)

    f_starts, f_ends, f_group = chunk_info(i)
    cb = nchunk - 1 - i
    b_starts, b_ends, b_group = chunk_info(cb)
    gates(xf_ref, xfp_ref, xfn_ref, f_starts, f_ends, 0, af_scr, bf_scr)
    gates(xb_ref, xbp_ref, xbn_ref, b_starts, b_ends, 1, ab_scr, bb_scr)

    @pl.when(f_starts)
    def _():
        sf_scr[...] = jnp.broadcast_to(h0f_ref[pl.ds(f_group, 1), :], sf_scr.shape)

    @pl.when(b_ends)
    def _():
        sb_scr[...] = jnp.broadcast_to(h0b_ref[pl.ds(b_group, 1), :], sb_scr.shape)

    ntile = ts // SUBLANES

    def tile_step(k, carry):
        hf, hb = carry
        base_f = pl.multiple_of(k * SUBLANES, SUBLANES)
        base_b = pl.multiple_of((ntile - 1 - k) * SUBLANES, SUBLANES)
        a_f = af_scr[pl.ds(base_f, SUBLANES), :]
        b_f = bf_scr[pl.ds(base_f, SUBLANES), :]
        a_b = ab_scr[pl.ds(base_b, SUBLANES), :]
        b_b = bb_scr[pl.ds(base_b, SUBLANES), :]
        out_f = jnp.zeros((SUBLANES, LRU_WIDTH), F32)
        out_b = jnp.zeros((SUBLANES, LRU_WIDTH), F32)
        for r in range(SUBLANES):
            rb = SUBLANES - 1 - r
            hf = a_f[r:r + 1, :] * hf + b_f[r:r + 1, :]
            hb = a_b[rb:rb + 1, :] * hb + b_b[rb:rb + 1, :]
            out_f = jnp.where(tile_row == r, hf, out_f)
            out_b = jnp.where(tile_row == rb, hb, out_b)
        hf_ref[pl.ds(base_f, SUBLANES), :] = out_f
        hb_ref[pl.ds(base_b, SUBLANES), :] = out_b
        return hf, hb

    hf, hb = lax.fori_loop(0, ntile, tile_step, (sf_scr[0:1, :], sb_scr[0:1, :]))
    sf_scr[...] = jnp.broadcast_to(hf, sf_scr.shape)
    sb_scr[...] = jnp.broadcast_to(hb, sb_scr.shape)


def _lru(z1, conv_w, conv_b, wa_hi, wa_lo, wx_hi, wx_lo, ba, bx, lam, h0f, h0b, *, nctx, dec_seq):
    n = z1.shape[0]
    ts = TOKEN_TILE
    nchunk = n // ts
    xcol = COL_XR // LRU_WIDTH
    tiles = ts // SUBLANES
    last_tile = n // SUBLANES - 1

    def cur(rev):
        return lambda i: ((nchunk - 1 - i) if rev else i, xcol)

    def prev(rev):
        return lambda i: (jnp.maximum(((nchunk - 1 - i) if rev else i) * tiles - 1, 0), xcol)

    def nxt(rev):
        return lambda i: (jnp.minimum((((nchunk - 1 - i) if rev else i) + 1) * tiles, last_tile), xcol)

    x_specs = []
    for rev in (False, True):
        x_specs += [pl.BlockSpec((ts, LRU_WIDTH), cur(rev)),
                    pl.BlockSpec((SUBLANES, LRU_WIDTH), prev(rev)),
                    pl.BlockSpec((SUBLANES, LRU_WIDTH), nxt(rev))]
    full = lambda shape: pl.BlockSpec(shape, lambda i: (0,) * len(shape))
    w_spec = full((2, LRU_WIDTH, LRU_WIDTH))
    v_spec = full((2, 1, LRU_WIDTH))
    out_shape = jax.ShapeDtypeStruct((n, LRU_WIDTH), F32)
    scr = pltpu.VMEM((ts, LRU_WIDTH), F32)
    state = pltpu.VMEM((SUBLANES, LRU_WIDTH), F32)
    return pl.pallas_call(
        functools.partial(_lru_kernel, nctx_blk=nctx // ts, seq_blk=dec_seq // ts),
        grid=(nchunk,),
        in_specs=x_specs + [full((4, LRU_WIDTH)), full((1, LRU_WIDTH)),
                            w_spec, w_spec, w_spec, w_spec, v_spec, v_spec, v_spec,
                            full((SUBLANES, LRU_WIDTH)), full((SUBLANES, LRU_WIDTH))],
        out_specs=[pl.BlockSpec((ts, LRU_WIDTH), lambda i: (i, 0)),
                   pl.BlockSpec((ts, LRU_WIDTH), lambda i: (nchunk - 1 - i, 0))],
        out_shape=[out_shape, out_shape],
        scratch_shapes=[scr, scr, scr, scr, state, state],
        compiler_params=_cparams(("arbitrary",)),
        name="lru",
    )(z1, z1, z1, z1, z1, z1, conv_w, conv_b, wa_hi, wa_lo, wx_hi, wx_lo, ba, bx, lam, h0f, h0b)


def _merge_kernel(x_ref, mod_ref, oa_ref, ob_ref, hf_ref, hb_ref, z2_ref,
                  woa_ref, wob_ref, woc_ref, wout_ref, o_ref):
    d = x_ref.shape[1]
    oc = (hf_ref[...] + hb_ref[...]) * _gelu(z2_ref[:, 0:LRU_WIDTH])
    ga = z2_ref[:, LRU_WIDTH:LRU_WIDTH + d]
    gb = z2_ref[:, LRU_WIDTH + d:LRU_WIDTH + 2 * d]
    gc = z2_ref[:, LRU_WIDTH + 2 * d:LRU_WIDTH + 3 * d]
    merged = (_sigmoid(ga) * _dot(oa_ref[...], woa_ref[...])
              + _sigmoid(gb) * _dot(ob_ref[...], wob_ref[...])
              + _sigmoid(gc) * _dot(oc.astype(BF16), woc_ref[...]))
    out = _dot(merged.astype(BF16), wout_ref[...])
    gate = mod_ref[0][:, 2 * d:3 * d]
    o_ref[...] = x_ref[...] + gate * out


def _merge(x, mod3, oa, ob, hf, hb, z2, wo_a, wo_b, wo_c, w_out, *, nctx, dec_seq):
    n, d = x.shape
    tm = TOKEN_TILE
    row = lambda w: pl.BlockSpec((tm, w), lambda i: (i, 0))
    full = lambda a: pl.BlockSpec(a.shape, lambda i: (0, 0))
    return pl.pallas_call(
        _merge_kernel,
        grid=(n // tm,),
        in_specs=[row(d),
                  pl.BlockSpec((1, 1, mod3.shape[2]),
                               lambda i: (_group_of_block(i, tm, nctx, dec_seq), 0, 0)),
                  row(Q_WIDTH), row(Q_WIDTH), row(LRU_WIDTH), row(LRU_WIDTH), row(Z2_WIDTH),
                  full(wo_a), full(wo_b), full(wo_c), full(w_out)],
        out_specs=row(d),
        out_shape=jax.ShapeDtypeStruct((n, d), F32),
        compiler_params=_cparams(("parallel",)),
        name="merge",
    )(x, mod3, oa, ob, hf, hb, z2, wo_a, wo_b, wo_c, w_out)


ORDER_SENTINEL = 1e9


def _candidate_groups():
    k, s = PEER_TOPK, SUBLANES
    split = 4
    groups, covered = [], set()
    for a in range(split):
        for b0 in range(0, k // (a + 1), s):
            groups.append((True, a, b0, 0))
            covered |= {(a, b) for b in range(b0, b0 + s)}
    for b in range(k // (split + 1)):
        for a0 in range(0, k // (b + 1), s):
            if a0 + s > split:
                groups.append((False, a0, b, max(split, a0)))
                covered |= {(a, b) for a in range(max(split, a0), a0 + s)}
    needed = {(a, b) for a in range(k) for b in range(k) if (a + 1) * (b + 1) <= k}
    assert needed <= covered and len(covered) == sum(
        s if f else a0 + s - amin for f, a0, _, amin in groups)
    return groups


class _TopRows:
    def __init__(self, s, extras=(), order=None):
        self.s = s
        self.rows = lax.broadcasted_iota(jnp.int32, s.shape, 0).astype(F32) if order is None else order
        self.extras = extras
        self.vals, self.idxs, self.picked = [], [], [[] for _ in extras]

    def round(self, anchor=None):
        m = jnp.max(self.s, axis=0, keepdims=True)
        if anchor is not None:
            m = jnp.maximum(m, anchor)
        pos = jnp.min(jnp.where(self.s == m, self.rows, ORDER_SENTINEL), axis=0, keepdims=True)
        sel = self.rows == pos
        self.vals.append(m)
        self.idxs.append(pos)
        for dst, ex in zip(self.picked, self.extras):
            dst.append(jnp.max(jnp.where(sel, ex, -1.0), axis=0, keepdims=True))
        self.s = jnp.where(sel, -jnp.inf, self.s)

    def run(self, k):
        for _ in range(k):
            self.round()
        return self

    def result(self):
        cat = lambda xs: jnp.concatenate(xs, axis=0)
        return cat(self.vals), cat(self.idxs), [cat(p) for p in self.picked]


def _subkey_tops(selectors):
    return [a for sel in selectors for a in sel.result()[:2]]


def _pair_gates(selector):
    best, _, (e1, e2) = selector.result()
    ex = jnp.exp(best - best[0:1])
    return ex / jnp.sum(ex, axis=0, keepdims=True), e1, e2


def _peer_topk_kernel(q_ref, khi_ref, klo_ref, g_ref, i1_ref, i2_ref):
    for t0 in range(0, q_ref.shape[0], LANES):
        subs = [_TopRows(s).run(PEER_TOPK) for s in _peer_scores(q_ref, khi_ref, klo_ref, t0)]
        picked = _pair_gates(_peer_pair_rows(*_subkey_tops(subs)).run(PEER_TOPK))
        g_ref[0, :, t0:t0 + LANES], i1_ref[0, :, t0:t0 + LANES], i2_ref[0, :, t0:t0 + LANES] = picked


def _peer_scores(q_ref, khi_ref, klo_ref, t0):
    scores = []
    for p in range(2):
        q_hi, q_lo = _split(q_ref[t0:t0 + LANES, p * PEER_DKEY:(p + 1) * PEER_DKEY])
        k_hi, k_lo = khi_ref[0, p], klo_ref[0, p]
        scores.append(_dot_nt(k_hi, q_hi) + _dot_nt(k_lo, q_hi) + _dot_nt(k_hi, q_lo))
    return scores


def _peer_pair_rows(vals0, idx0, vals1, idx1):
    top_s, top_i = (vals0, vals1), (idx0, idx1)
    tb = top_s[0].shape[1]
    gshape = (SUBLANES, tb)
    sub_row = lax.broadcasted_iota(jnp.int32, gshape, 0).astype(F32)
    cand_s, cand_1, cand_2, order = [], [], [], []
    for fixed_first, a, b, a_min in _candidate_groups():
        if fixed_first:
            bcast = lambda x: jnp.broadcast_to(x[a:a + 1], gshape)
            cand_s.append(bcast(top_s[0]) + top_s[1][b:b + SUBLANES])
            cand_1.append(bcast(top_i[0]))
            cand_2.append(top_i[1][b:b + SUBLANES])
            order.append(float(a * PEER_TOPK + b) + sub_row)
        else:
            bcast = lambda x: jnp.broadcast_to(x[b:b + 1], gshape)
            live = sub_row >= float(a_min - a)
            cand_s.append(jnp.where(live, top_s[0][a:a + SUBLANES] + bcast(top_s[1]), -jnp.inf))
            cand_1.append(top_i[0][a:a + SUBLANES])
            cand_2.append(bcast(top_i[1]))
            order.append(jnp.where(live, float(a * PEER_TOPK + b) + float(PEER_TOPK) * sub_row,
                                   ORDER_SENTINEL))
    cat = lambda xs: jnp.concatenate(xs, axis=0)
    return _TopRows(cat(cand_s), (cat(cand_1), cat(cand_2)), cat(order))


def _peer_topk(q, keys_hi, keys_lo):
    n = q.shape[0]
    tb = PEER_TB
    spec = pl.BlockSpec((1, PEER_TOPK, tb), lambda i, h: (h, 0, i))
    shape = jax.ShapeDtypeStruct((PEER_HEADS, PEER_TOPK, n), F32)
    k_spec = pl.BlockSpec((1, 2, PEER_NKEYS, PEER_DKEY), lambda i, h: (h, 0, 0, 0))
    return pl.pallas_call(
        _peer_topk_kernel,
        grid=(n // tb, PEER_HEADS),
        in_specs=[pl.BlockSpec((tb, 2 * PEER_DKEY), lambda i, h: (i, h)), k_spec, k_spec],
        out_specs=[spec, spec, spec],
        out_shape=[shape, shape, shape],
        compiler_params=_cparams(("parallel", "parallel")),
        name="peer_topk",
    )(q, keys_hi, keys_lo)


def _peer_expert_kernel(x_ref, mod_ref, gn_ref, first_g_ref, first_i1_ref, first_i2_ref,
                        qn_ref, khi_ref, klo_ref, ut_ref, v_ref, o_ref,
                        h_scr, w_scr, p_scr, acc_scr, sub_scr, pair_scr, tok_scr):
    i = pl.program_id(0)
    c = pl.program_id(1)
    tb, d = x_ref.shape
    ec = ut_ref.shape[1]
    sub = ec // PEER_NKEYS
    npair = PEER_HEADS * PEER_TOPK
    g_ref, i1_ref, i2_ref = tok_scr.at[0], tok_scr.at[1], tok_scr.at[2]

    tiles = range(0, tb, LANES)

    def pair_selectors():
        return [_peer_pair_rows(*[sub_scr[k, :, t0:t0 + LANES] for k in range(4)]) for t0 in tiles]

    def store_pairs(head, selectors):
        rows = pl.ds(pl.multiple_of(head * PEER_TOPK, PEER_TOPK), PEER_TOPK)
        for t0, sel in zip(tiles, selectors):
            for k, a in enumerate(_pair_gates(sel)):
                pair_scr[k, rows, t0:t0 + LANES] = a

    @pl.when(jnp.logical_and(c == 0, i == 0))
    def _():
        sub_scr[...] = jnp.zeros(sub_scr.shape, F32)
        for k, ref in enumerate((first_g_ref, first_i1_ref, first_i2_ref)):
            tok_scr[k] = ref[...].reshape(npair, tb).T

    @pl.when(jnp.logical_and(c == 0, i > 0))
    def _():
        store_pairs(PEER_HEADS - 1, [sel.run(PEER_TOPK) for sel in pair_selectors()])
        for k in range(3):
            tok_scr[k] = pair_scr[k].T

    @pl.when(c == 0)
    def _():
        x = x_ref[...]
        y = x * lax.rsqrt(jnp.mean(x * x, axis=-1, keepdims=True) + EPS) * gn_ref[...]
        m = mod_ref[0]
        h_scr[...] = (y * (1.0 + m[:, 4 * d:5 * d]) + m[:, 3 * d:4 * d]).astype(BF16)
        acc_scr[...] = jnp.zeros(acc_scr.shape, F32)
        key_row = lax.broadcasted_iota(jnp.int32, (PEER_NKEYS, npair), 0).astype(F32)

        def token_group(tg, carry):
            for u in range(PEER_TOKEN_UNROLL):
                t = tg * PEER_TOKEN_UNROLL + u
                gate = g_ref[pl.ds(t, 1), :]
                ga = jnp.where(key_row == i1_ref[pl.ds(t, 1), :], gate, 0.0).astype(BF16)
                ob = jnp.where(key_row == i2_ref[pl.ds(t, 1), :], 1.0, 0.0).astype(BF16)
                w_scr[pl.ds(pl.multiple_of(t * PEER_W_PITCH, SUBLANES), PEER_NKEYS), :] = _dot_nt(ga, ob)
            return carry

        lax.fori_loop(0, tb // PEER_TOKEN_UNROLL, token_group, 0)

    pair_sel = pair_selectors()
    sub_sel = [[_TopRows(s) for s in _peer_scores(qn_ref, khi_ref, klo_ref, t0)] for t0 in tiles]
    rounds = []
    for _ in range(PEER_TOPK):
        rounds.append([sel for two in sub_sel for sel in two])
        rounds.append(pair_sel)
    nblock = ec // PEER_BLOCK
    per_block = -(-len(rounds) // nblock)

    def advance(h_b):
        anchor = jnp.minimum(h_b[0:1, 0:LANES], -jnp.inf)
        for n, sels in enumerate(rounds[:per_block]):
            for sel in sels:
                sel.round(anchor if n == 0 else None)
        del rounds[:per_block]

    def activate(b, h_b):
        for s in range(b * PEER_BLOCK // PEER_NKEYS, (b + 1) * PEER_BLOCK // PEER_NKEYS):
            w = w_scr[pl.ds(c * sub + s, tb, stride=PEER_W_PITCH), :]
            lo = s * PEER_NKEYS - b * PEER_BLOCK
            x = h_b[:, lo:lo + PEER_NKEYS]
            t = jnp.tanh(_gelu_inner(x).astype(BF16))
            p_scr[:, s * PEER_NKEYS:(s + 1) * PEER_NKEYS] = (0.5 * w).astype(BF16) * (x.astype(BF16) * (1.0 + t))
        return _dot(p_scr[:, b * PEER_BLOCK:(b + 1) * PEER_BLOCK], v_ref[b * PEER_BLOCK:(b + 1) * PEER_BLOCK, :])

    acc = acc_scr[...]
    h_prev = None
    for b in range(nblock):
        h_b = _dot(h_scr[...], ut_ref[:, b * PEER_BLOCK:(b + 1) * PEER_BLOCK])
        advance(h_b)
        if h_prev is not None:
            acc = acc + activate(b - 1, h_prev)
        h_prev = h_b
    acc_scr[...] = acc + activate(nblock - 1, h_prev)
    assert not rounds

    store_pairs(jnp.where(c == 0, PEER_HEADS - 1, c - 1), pair_sel)
    for t0, two in zip(tiles, sub_sel):
        for k, a in enumerate(_subkey_tops(two)):
            sub_scr[k, :, t0:t0 + LANES] = a

    @pl.when(c == pl.num_programs(1) - 1)
    def _():
        o_ref[...] = x_ref[...] + mod_ref[0][:, 5 * d:6 * d] * acc_scr[...]


def _peer_expert(x, mod3, gain, q, keys_hi, keys_lo, ut, v, *, nctx, dec_seq):
    n, d = x.shape
    tb, ec = PEER_TB, PEER_EC
    nexp = v.shape[0]
    ntile = n // tb
    npair = PEER_HEADS * PEER_TOPK
    assert nexp // ec == PEER_HEADS
    first = _peer_topk(q[:tb], keys_hi, keys_lo)
    row = lambda w: pl.BlockSpec((tb, w), lambda i, c: (i, 0))
    first_spec = pl.BlockSpec((PEER_HEADS, PEER_TOPK, tb), lambda i, c: (0, 0, 0))
    k_spec = pl.BlockSpec((1, 2, PEER_NKEYS, PEER_DKEY), lambda i, c: (c, 0, 0, 0))
    return pl.pallas_call(
        _peer_expert_kernel,
        grid=(ntile, nexp // ec),
        in_specs=[row(d),
                  pl.BlockSpec((1, 1, mod3.shape[2]),
                               lambda i, c: (_group_of_block(i, tb, nctx, dec_seq), 0, 0)),
                  pl.BlockSpec((1, d), lambda i, c: (0, 0)),
                  first_spec, first_spec, first_spec,
                  pl.BlockSpec((tb, 2 * PEER_DKEY), lambda i, c: (jnp.minimum(i + 1, ntile - 1), c)),
                  k_spec, k_spec,
                  pl.BlockSpec((d, ec), lambda i, c: (0, c)),
                  pl.BlockSpec((ec, d), lambda i, c: (c, 0))],
        out_specs=row(d),
        out_shape=jax.ShapeDtypeStruct((n, d), F32),
        scratch_shapes=[pltpu.VMEM((tb, d), BF16),
                        pltpu.VMEM((tb * PEER_W_PITCH, PEER_NKEYS), F32),
                        pltpu.VMEM((tb, ec), BF16),
                        pltpu.VMEM((tb, d), F32),
                        pltpu.VMEM((4, PEER_TOPK, tb), F32),
                        pltpu.VMEM((3, npair, tb), F32),
                        pltpu.VMEM((3, tb, npair), F32)],
        compiler_params=_cparams(("arbitrary", "arbitrary")),
        name="peer_expert",
    )(x, mod3, gain.reshape(1, d), *first, q, keys_hi, keys_lo, ut, v)


def _final_norm_kernel(x_ref, g_ref, o_ref):
    x = x_ref[...]
    o_ref[...] = x * lax.rsqrt(jnp.mean(x * x, axis=-1, keepdims=True) + EPS) * g_ref[...]


def _final_norm(x, gain):
    n, d = x.shape
    tm = 512
    return pl.pallas_call(
        _final_norm_kernel,
        grid=(n // tm,),
        in_specs=[pl.BlockSpec((tm, d), lambda i: (i, 0)), pl.BlockSpec((1, d), lambda i: (0, 0))],
        out_specs=pl.BlockSpec((tm, d), lambda i: (i, 0)),
        out_shape=jax.ShapeDtypeStruct((n, d), F32),
        compiler_params=_cparams(("parallel",)),
        name="final_norm",
    )(x, gain.reshape(1, d))


def _rope_tables(dec_seq, tile):
    t = jnp.arange(dec_seq)
    n_freq = HEAD_DIM // 4
    inv = ROPE_BASE ** (-jnp.arange(n_freq, dtype=F32) / n_freq)
    ang = jnp.concatenate([(t // GRID_W).astype(F32)[:, None] * inv,
                           (t % GRID_W).astype(F32)[:, None] * inv], axis=-1)
    cos, sin = jnp.cos(ang), jnp.sin(ang)
    reps = LANES // HEAD_DIM
    cos_t = jnp.tile(jnp.concatenate([cos, cos], axis=-1), (1, reps))
    sin_t = jnp.tile(jnp.concatenate([-sin, sin], axis=-1), (1, reps))
    cos_t = jnp.concatenate([jnp.ones((tile, LANES), F32), cos_t], axis=0)
    sin_t = jnp.concatenate([jnp.zeros((tile, LANES), F32), sin_t], axis=0)
    return cos_t, sin_t


def _dup_cache(cache):
    c = jnp.transpose(cache, (0, 2, 1, 3))
    return jnp.concatenate([c, c], axis=-1).astype(BF16)


def _dense_blockdiag(w):
    dirs, nb, bw, _ = w.shape
    eye = jnp.eye(nb, dtype=w.dtype)
    return jnp.einsum('dncf,nm->dncmf', w, eye).reshape(dirs, nb * bw, nb * bw)


def _pad_rows(a, rows):
    return jnp.concatenate([a, jnp.zeros((rows - a.shape[0],) + a.shape[1:], a.dtype)], axis=0)


def kernel(x_prompt, x_sample, c, cache_wa_k, cache_wa_v, cache_ax_k, cache_ax_v, state_lru_fwd,
           state_lru_bwd, c_ctx, w_ada, b_ada, g_norm1, w_in, wa_sink, ax_q_gain, ax_k_gain, conv_w,
           conv_b, lru_wa, lru_ba, lru_wx, lru_bx, lru_lambda, wo_a, wo_b, wo_c, w_out, g_norm2,
           peer_wq, peer_keys, peer_u, peer_v, g_final):
    batch, seq, d = x_prompt.shape
    dec_batch, dec_seq, _ = x_sample.shape
    depth = w_in.shape[0]
    nctx = batch * seq
    nlat = dec_batch * dec_seq
    assert seq == TOKEN_TILE and dec_seq % TOKEN_TILE == 0 and 1 + dec_batch <= SUBLANES
    assert w_in.shape[2] == Z1_WIDTH + Z2_WIDTH and d == 1024
    sizes = dict(nctx=nctx, dec_seq=dec_seq)

    x = jnp.concatenate([x_prompt.reshape(nctx, d), x_sample.reshape(nlat, d)], axis=0)
    cvec = _pad_rows(jnp.concatenate([c_ctx[None, :], c], axis=0), SUBLANES)
    mods = _ada(cvec, w_ada, b_ada)

    cos_tab, sin_tab = _rope_tables(dec_seq, TOKEN_TILE)
    head_mean = jnp.kron(jnp.eye(LANES // HEAD_DIM, dtype=F32),
                         jnp.full((HEAD_DIM, HEAD_DIM), 1.0 / HEAD_DIM, F32)).astype(BF16)
    reps = LANES // HEAD_DIM

    new_wa_k, new_wa_v, new_ax_k, new_ax_v, new_hf, new_hb = [], [], [], [], [], []
    for l in range(depth):
        mod3 = mods[l].reshape(SUBLANES, 1, 6 * d)
        w_in_bf = w_in[l].astype(BF16)
        z1 = _modmm(x, mod3, g_norm1[l], w_in_bf[:, :Z1_WIDTH], None, shift_col=0, tn=Z1_WIDTH, **sizes)
        z2 = _modmm(x, mod3, g_norm1[l], w_in_bf[:, Z1_WIDTH:], None, shift_col=0, tn=Z2_WIDTH, **sizes)

        gq = jnp.tile(ax_q_gain[l], reps).reshape(1, LANES)
        gk = jnp.tile(ax_k_gain[l], reps).reshape(1, LANES)
        qsa, kka, vva, qsb, kkb, vvb, kbn, qtb = _prep(z1, cos_tab, sin_tab, gq, gk, head_mean, **sizes)

        oa_ctx, ob_ctx = _att_ctx(wa_sink[l], qsa, kka, vva, qsb, kkb, vvb, nseq=batch, seq=seq)
        oa_lat = _att_win(wa_sink[l], qsa, kka, vva, _dup_cache(cache_wa_k[:, l]),
                          _dup_cache(cache_wa_v[:, l]), nctx=nctx, dec_batch=dec_batch, dec_seq=dec_seq)

        def with_cache(cache, cur):
            lat = cur[:, nctx:].reshape(KV_HEADS, dec_batch, dec_seq, LANES)
            return jnp.concatenate([_dup_cache(cache), jnp.transpose(lat, (1, 0, 2, 3))], axis=2)

        vt = jnp.swapaxes(with_cache(cache_ax_v[:, l], vvb), 2, 3)
        vt2 = jnp.stack([vt.at[:, :, HEAD_DIM].set(1.0), vt.at[:, :, 0].set(1.0)], axis=1)
        ob_lat = _att_dense(qtb, with_cache(cache_ax_k[:, l], kkb), vt2,
                            nctx=nctx, dec_batch=dec_batch, dec_seq=dec_seq)
        oa = jnp.concatenate([oa_ctx, oa_lat], axis=0)
        ob = jnp.concatenate([ob_ctx, ob_lat], axis=0)

        wa_hi, wa_lo = _split(_dense_blockdiag(lru_wa[l]))
        wx_hi, wx_lo = _split(_dense_blockdiag(lru_wx[l]))
        vec = lambda a: a.reshape(2, 1, LRU_WIDTH)
        h0f = _pad_rows(jnp.concatenate([jnp.zeros((1, LRU_WIDTH), F32), state_lru_fwd[:, l]], axis=0), SUBLANES)
        h0b = _pad_rows(jnp.concatenate([jnp.zeros((1, LRU_WIDTH), F32), state_lru_bwd[:, l]], axis=0), SUBLANES)
        hf, hb = _lru(z1, conv_w[l], conv_b[l].reshape(1, LRU_WIDTH), wa_hi, wa_lo, wx_hi, wx_lo,
                      vec(lru_ba[l]), vec(lru_bx[l]), vec(lru_lambda[l]), h0f, h0b, **sizes)

        x = _merge(x, mod3, oa, ob, hf, hb, z2, wo_a[l].astype(BF16), wo_b[l].astype(BF16),
                   wo_c[l].astype(BF16), w_out[l].astype(BF16), **sizes)

        wq_hi, wq_lo = _split(peer_wq[l])
        q = _modmm(x, mod3, g_norm2[l], wq_hi, wq_lo, shift_col=3 * d, tn=wq_hi.shape[1], **sizes)
        keys_hi, keys_lo = _split(peer_keys[l])
        x = _peer_expert(x, mod3, g_norm2[l], q, keys_hi, keys_lo,
                         peer_u[l].T.astype(BF16), peer_v[l].astype(BF16), **sizes)

        ctx4 = lambda a: a[:nctx].reshape(batch, seq, KV_HEADS, HEAD_DIM)
        new_wa_k.append(ctx4(z1[:, COL_KA:COL_KA + KV_WIDTH]))
        new_wa_v.append(ctx4(z1[:, COL_VA:COL_VA + KV_WIDTH]))
        new_ax_k.append(ctx4(kbn))
        new_ax_v.append(ctx4(z1[:, COL_VB:COL_VB + KV_WIDTH]))
        new_hf.append(hf[:nctx].reshape(batch, seq, LRU_WIDTH)[:, -1])
        new_hb.append(hb[:nctx].reshape(batch, seq, LRU_WIDTH)[:, 0])

    y = _final_norm(x, g_final)
    stack = lambda xs: jnp.stack(xs, axis=1)
    return (y[:nctx].reshape(batch, seq, d), y[nctx:].reshape(dec_batch, dec_seq, d),
            stack(new_wa_k), stack(new_wa_v), stack(new_ax_k), stack(new_ax_v),
            stack(new_hf), stack(new_hb))
```

```python
import functools

import jax
import jax.numpy as jnp
from jax import lax
from jax.experimental import pallas as pl
from jax.experimental.pallas import tpu as pltpu

F32 = jnp.float32
BF16 = jnp.bfloat16

HEAD_DIM = 64
KV_HEADS = 2
Q_GROUP = 4
Q_WIDTH = KV_HEADS * Q_GROUP * HEAD_DIM
KV_WIDTH = KV_HEADS * HEAD_DIM
WINDOW = 128
GRID_W = 64
LRU_WIDTH = 512
LRU_BLOCKS = 8
LRU_C = 8.0
PEER_HEADS = 8
PEER_NKEYS = 128
PEER_DKEY = 128
PEER_TOPK = 16
ROPE_BASE = 10000.0
EPS = 1e-6
NEG = -1e30
LANES = 128
SUBLANES = 8
VMEM_LIMIT = 56 * 1024 * 1024

Z1_WIDTH = 2048
Z2_WIDTH = 3584
COL_KA, COL_VA, COL_QB, COL_KB, COL_VB, COL_XR = 512, 640, 768, 1280, 1408, 1536

TOKEN_TILE = 256
ATT_A_TQ = 128
ATT_B_TQ = 256
ATT_B_TK = (1536, 1024, 512)
PEER_TB = 256
PEER_EC = 2048
PEER_BLOCK = 256
PEER_TOKEN_UNROLL = 64
PEER_W_PITCH = PEER_NKEYS + SUBLANES


def _cparams(sem):
    return pltpu.CompilerParams(dimension_semantics=sem, vmem_limit_bytes=VMEM_LIMIT)


def _split(x):
    hi = x.astype(BF16)
    lo = (x - hi.astype(F32)).astype(BF16)
    return hi, lo


def _dot(a, b):
    return jnp.dot(a, b, preferred_element_type=F32)


def _dot_nt(a, b):
    return lax.dot_general(a, b, (((1,), (1,)), ((), ())), preferred_element_type=F32)


def _sigmoid(x):
    return 1.0 / (1.0 + jnp.exp(-x))


def _gelu_inner(x):
    return 0.7978845608028654 * (x + 0.044715 * (x * x * x))


def _gelu(x):
    return 0.5 * x * (1.0 + jnp.tanh(_gelu_inner(x)))


def _group_of_block(i, tile, nctx, dec_seq):
    row = i * tile
    return jnp.where(row < nctx, 0, 1 + (row - nctx) // dec_seq)


def _ada_kernel(c_ref, w_ref, b_ref, o_ref):
    c = c_ref[...]
    s = c * _sigmoid(c)
    s_hi, s_lo = _split(s)
    w_hi, w_lo = _split(w_ref[0])
    o_ref[0] = _dot(s_hi, w_hi) + _dot(s_lo, w_hi) + _dot(s_hi, w_lo) + b_ref[0]


def _ada(cvec, w_ada, b_ada):
    depth, d, e = w_ada.shape
    tn = 1536
    return pl.pallas_call(
        _ada_kernel,
        grid=(depth, e // tn),
        in_specs=[
            pl.BlockSpec((SUBLANES, d), lambda l, j: (0, 0)),
            pl.BlockSpec((1, d, tn), lambda l, j: (l, 0, j)),
            pl.BlockSpec((1, 1, tn), lambda l, j: (l, 0, j)),
        ],
        out_specs=pl.BlockSpec((1, SUBLANES, tn), lambda l, j: (l, 0, j)),
        out_shape=jax.ShapeDtypeStruct((depth, SUBLANES, e), F32),
        compiler_params=_cparams(("parallel", "parallel")),
        name="ada",
    )(cvec, w_ada, b_ada.reshape(depth, 1, e))


def _modmm_kernel(x_ref, mod_ref, g_ref, w_ref, *rest, shift_col, three_pass):
    d = x_ref.shape[1]
    if three_pass:
        wlo_ref, o_ref, h_scr, hlo_scr = rest
    else:
        o_ref, h_scr = rest

    @pl.when(pl.program_id(1) == 0)
    def _():
        x = x_ref[...]
        y = x * lax.rsqrt(jnp.mean(x * x, axis=-1, keepdims=True) + EPS) * g_ref[...]
        m = mod_ref[0]
        h = y * (1.0 + m[:, shift_col + d:shift_col + 2 * d]) + m[:, shift_col:shift_col + d]
        hi = h.astype(BF16)
        h_scr[...] = hi
        if three_pass:
            hlo_scr[...] = (h - hi.astype(F32)).astype(BF16)

    acc = _dot(h_scr[...], w_ref[...])
    if three_pass:
        acc = acc + _dot(hlo_scr[...], w_ref[...]) + _dot(h_scr[...], wlo_ref[...])
    o_ref[...] = acc


def _modmm(x, mod3, gain, w, w_lo, *, shift_col, tn, nctx, dec_seq):
    n, d = x.shape
    width = w.shape[1]
    tm = 512
    three_pass = w_lo is not None
    grp = lambda i, j: (_group_of_block(i, tm, nctx, dec_seq), 0, 0)
    in_specs = [
        pl.BlockSpec((tm, d), lambda i, j: (i, 0)),
        pl.BlockSpec((1, 1, mod3.shape[2]), grp),
        pl.BlockSpec((1, d), lambda i, j: (0, 0)),
        pl.BlockSpec((d, tn), lambda i, j: (0, j)),
    ]
    args = [x, mod3, gain.reshape(1, d), w]
    scratch = [pltpu.VMEM((tm, d), BF16)]
    if three_pass:
        in_specs.append(pl.BlockSpec((d, tn), lambda i, j: (0, j)))
        args.append(w_lo)
        scratch.append(pltpu.VMEM((tm, d), BF16))
    return pl.pallas_call(
        functools.partial(_modmm_kernel, shift_col=shift_col, three_pass=three_pass),
        grid=(n // tm, width // tn),
        in_specs=in_specs,
        out_specs=pl.BlockSpec((tm, tn), lambda i, j: (i, j)),
        out_shape=jax.ShapeDtypeStruct((n, width), F32),
        scratch_shapes=scratch,
        compiler_params=_cparams(("parallel", "arbitrary")),
        name="modmm",
    )(*args)


def _prep_kernel(z_ref, cos_ref, sin_ref, gq_ref, gk_ref, m_ref,
                 qsa_ref, kka_ref, vva_ref, qsb_ref, kkb_ref, vvb_ref, kbn_ref, qta_ref, qtb_ref):
    tm = z_ref.shape[0]
    lane = lax.broadcasted_iota(jnp.int32, (tm, LANES), 1)
    low_head = lane < HEAD_DIM
    first_half = (lane & (HEAD_DIM // 2)) == 0
    cos = cos_ref[...]
    sin = sin_ref[...]
    mmat = m_ref[...]
    scale = HEAD_DIM ** -0.5

    def rope(x):
        back = pltpu.roll(x, HEAD_DIM // 2, 1)
        fwd = pltpu.roll(x, LANES - HEAD_DIM // 2, 1)
        return x * cos + jnp.where(first_half, fwd, back) * sin

    def headnorm(x, gain):
        sq_hi, sq_lo = _split(x * x)
        ms = _dot(sq_hi, mmat) + _dot(sq_lo, mmat)
        return x * lax.rsqrt(ms + EPS) * gain

    def store_q(ref, c, q, t_ref=None):
        zero = jnp.zeros_like(q)
        j, g0 = c // 2, 2 * (c % 2)
        for g, qm in ((g0, jnp.where(low_head, q, zero)), (g0 + 1, jnp.where(low_head, zero, q))):
            ref[j, g] = qm.astype(BF16)
            if t_ref is not None:
                t_ref[j, g] = qm.T.astype(BF16)

    def store_dup(ref, x):
        swapped = pltpu.roll(x, HEAD_DIM, 1)
        ref[0] = jnp.where(low_head, x, swapped).astype(BF16)
        ref[1] = jnp.where(low_head, swapped, x).astype(BF16)

    for c in range(Q_WIDTH // LANES):
        store_q(qsa_ref, c, rope(z_ref[:, c * LANES:(c + 1) * LANES]) * scale, qta_ref)
        qb = headnorm(z_ref[:, COL_QB + c * LANES:COL_QB + (c + 1) * LANES], gq_ref[...])
        store_q(qsb_ref, c, rope(qb) * scale, qtb_ref)
    store_dup(kka_ref, rope(z_ref[:, COL_KA:COL_KA + KV_WIDTH]))
    store_dup(vva_ref, z_ref[:, COL_VA:COL_VA + KV_WIDTH])
    kb = rope(headnorm(z_ref[:, COL_KB:COL_KB + KV_WIDTH], gk_ref[...]))
    kbn_ref[...] = kb
    store_dup(kkb_ref, kb)
    store_dup(vvb_ref, z_ref[:, COL_VB:COL_VB + KV_WIDTH])


def _prep(z1, cos_tab, sin_tab, gq, gk, mmat, *, nctx, dec_seq):
    n = z1.shape[0]
    tm = TOKEN_TILE
    nctx_blk = nctx // tm
    seq_blk = dec_seq // tm

    def tab_idx(i):
        return (jnp.where(i < nctx_blk, 0, 1 + (i - nctx_blk) % seq_blk), 0)

    qs_spec = pl.BlockSpec((KV_HEADS, Q_GROUP, tm, LANES), lambda i: (0, 0, i, 0))
    kv_spec = pl.BlockSpec((KV_HEADS, tm, LANES), lambda i: (0, i, 0))
    qs_shape = jax.ShapeDtypeStruct((KV_HEADS, Q_GROUP, n, LANES), BF16)
    qt_spec = pl.BlockSpec((KV_HEADS, Q_GROUP, LANES, tm), lambda i: (0, 0, 0, i))
    qt_shape = jax.ShapeDtypeStruct((KV_HEADS, Q_GROUP, LANES, n), BF16)
    kv_shape = jax.ShapeDtypeStruct((KV_HEADS, n, LANES), BF16)
    return pl.pallas_call(
        _prep_kernel,
        grid=(n // tm,),
        in_specs=[
            pl.BlockSpec((tm, Z1_WIDTH), lambda i: (i, 0)),
            pl.BlockSpec((tm, LANES), tab_idx),
            pl.BlockSpec((tm, LANES), tab_idx),
            pl.BlockSpec((1, LANES), lambda i: (0, 0)),
            pl.BlockSpec((1, LANES), lambda i: (0, 0)),
            pl.BlockSpec((LANES, LANES), lambda i: (0, 0)),
        ],
        out_specs=[qs_spec, kv_spec, kv_spec, qs_spec, kv_spec, kv_spec,
                   pl.BlockSpec((tm, LANES), lambda i: (i, 0)), qt_spec, qt_spec],
        out_shape=[qs_shape, kv_shape, kv_shape, qs_shape, kv_shape, kv_shape,
                   jax.ShapeDtypeStruct((n, LANES), F32), qt_shape, qt_shape],
        compiler_params=_cparams(("parallel",)),
        name="prep",
    )(z1, cos_tab, sin_tab, gq, gk, mmat)


def _merge_heads(o, tq, c):
    lane = lax.broadcasted_iota(jnp.int32, (tq, LANES), 1)
    g = 2 * c
    return jnp.where(lane < HEAD_DIM, o[g * tq:(g + 1) * tq], o[(g + 1) * tq:(g + 2) * tq])


def _sink_column(sink_ref, j, tq):
    return jnp.concatenate(
        [jnp.full((tq, 1), sink_ref[j * Q_GROUP + g], F32) for g in range(Q_GROUP)], axis=0)


def _att_ctx_kernel(sink_ref, qsa_ref, kka_ref, vva_ref, qsb_ref, kkb_ref, vvb_ref, oa_ref, ob_ref):
    tq = qsa_ref.shape[2]
    mixers = ((qsa_ref, kka_ref, vva_ref, oa_ref, True), (qsb_ref, kkb_ref, vvb_ref, ob_ref, False))
    scores = [[_dot_nt(q_ref[j].reshape(Q_GROUP * tq, LANES), k_ref[j]) for j in range(KV_HEADS)]
              for q_ref, k_ref, _, _, _ in mixers]
    for (_, _, v_ref, o_ref, has_sink), mixer_scores in zip(mixers, scores):
        for j in range(KV_HEADS):
            s = mixer_scores[j]
            m = jnp.max(s, axis=-1, keepdims=True)
            if has_sink:
                sk = _sink_column(sink_ref, j, tq)
                m = jnp.maximum(m, sk)
            e = jnp.exp(s - m)
            den = jnp.sum(e, axis=-1, keepdims=True)
            if has_sink:
                den = den + jnp.exp(sk - m)
            o = _dot(e.astype(BF16), v_ref[j]) / den
            for c in range(2):
                col = (2 * j + c) * LANES
                o_ref[:, col:col + LANES] = _merge_heads(o, tq, c).astype(BF16)


def _att_ctx(sink, qsa, kka, vva, qsb, kkb, vvb, *, nseq, seq):
    qs_spec = pl.BlockSpec((KV_HEADS, Q_GROUP, seq, LANES), lambda i: (0, 0, i, 0))
    kv_spec = pl.BlockSpec((KV_HEADS, seq, LANES), lambda i: (0, i, 0))
    o_spec = pl.BlockSpec((seq, Q_WIDTH), lambda i: (i, 0))
    o_shape = jax.ShapeDtypeStruct((nseq * seq, Q_WIDTH), BF16)
    return pl.pallas_call(
        _att_ctx_kernel,
        grid=(nseq,),
        in_specs=[pl.BlockSpec(memory_space=pltpu.SMEM),
                  qs_spec, kv_spec, kv_spec, qs_spec, kv_spec, kv_spec],
        out_specs=[o_spec, o_spec],
        out_shape=[o_shape, o_shape],
        compiler_params=_cparams(("parallel",)),
        name="att_ctx",
    )(sink, qsa, kka, vva, qsb, kkb, vvb)


def _att_win_kernel(sink_ref, qt_ref, kp_ref, kc_ref, kn_ref, vtp_ref, vtc_ref, vtn_ref,
                    kx_ref, vxt_ref, o_ref, *, nblk):
    i = pl.program_id(1)
    tq = qt_ref.shape[3]
    j_idx = lax.broadcasted_iota(jnp.int32, (tq, tq), 0)
    a_idx = lax.broadcasted_iota(jnp.int32, (tq, tq), 1)
    valid_prev = (j_idx >= a_idx) & (i > 0)
    valid_next = (j_idx <= a_idx) & (i < nblk - 1)
    k_refs = (kx_ref, kp_ref, kc_ref, kn_ref)
    vt_refs = (vxt_ref, vtp_ref, vtc_ref, vtn_ref)
    heads = [(j, g) for j in range(KV_HEADS) for g in range(Q_GROUP)]
    raw = [[_dot(k_ref[j], qt_ref[j, g]) for k_ref in k_refs] for j, g in heads]
    outs = []
    for h, (j, g) in enumerate(heads):
        s_x, s_p, s_c, s_n = raw[h]
        blocks = (s_x, jnp.where(valid_prev, s_p, NEG), s_c, jnp.where(valid_next, s_n, NEG))
        sk = sink_ref[h]
        m = jnp.maximum(jnp.max(blocks[0], axis=0, keepdims=True), sk)
        for s in blocks[1:]:
            m = jnp.maximum(m, jnp.max(s, axis=0, keepdims=True))
        den = jnp.exp(sk - m)
        o = jnp.zeros((LANES, tq), F32)
        for s, vt_ref in zip(blocks, vt_refs):
            e = jnp.exp(s - m)
            den = den + jnp.sum(e, axis=0, keepdims=True)
            o = o + _dot(vt_ref[j], e.astype(BF16))
        outs.append((o / den).T)
    lane = lax.broadcasted_iota(jnp.int32, (tq, LANES), 1)
    for c in range(len(heads) // 2):
        o_ref[:, c * LANES:(c + 1) * LANES] = jnp.where(
            lane < HEAD_DIM, outs[2 * c], outs[2 * c + 1]).astype(BF16)


def _att_win(sink, qt, kk, vvt, kx, vxt, *, nctx, dec_batch, dec_seq):
    tq = ATT_A_TQ
    assert tq == WINDOW
    nblk = dec_seq // tq
    base = nctx // tq
    past = kx.shape[2]

    def blk(b, i, delta):
        return base + b * nblk + jnp.clip(i + delta, 0, nblk - 1)

    k_specs = [pl.BlockSpec((KV_HEADS, tq, LANES), lambda b, i, d=d: (0, blk(b, i, d), 0)) for d in (-1, 0, 1)]
    vt_specs = [pl.BlockSpec((KV_HEADS, LANES, tq), lambda b, i, d=d: (0, 0, blk(b, i, d))) for d in (-1, 0, 1)]
    return pl.pallas_call(
        functools.partial(_att_win_kernel, nblk=nblk),
        grid=(dec_batch, nblk),
        in_specs=[pl.BlockSpec(memory_space=pltpu.SMEM),
                  pl.BlockSpec((KV_HEADS, Q_GROUP, LANES, tq), lambda b, i: (0, 0, 0, blk(b, i, 0)))]
                 + k_specs + vt_specs
                 + [pl.BlockSpec((None, KV_HEADS, past, LANES), lambda b, i: (b, 0, 0, 0)),
                    pl.BlockSpec((None, KV_HEADS, LANES, past), lambda b, i: (b, 0, 0, 0))],
        out_specs=pl.BlockSpec((tq, Q_WIDTH), lambda b, i: (b * nblk + i, 0)),
        out_shape=jax.ShapeDtypeStruct((dec_batch * dec_seq, Q_WIDTH), BF16),
        compiler_params=_cparams(("parallel", "parallel")),
        name="att_win",
    )(sink, qt, kk, kk, kk, vvt, vvt, vvt, kx, vxt)


def _att_dense_kernel(qt_ref, k_ref, vt_ref, o_ref, m_scr, acc_scr):
    kc = pl.program_id(2)
    tq = qt_ref.shape[3]

    @pl.when(kc == 0)
    def _():
        m_scr[...] = jnp.full(m_scr.shape, -jnp.inf, F32)
        acc_scr[...] = jnp.zeros(acc_scr.shape, F32)

    scores = [_dot(k_ref[j], qt_ref[j, g]) for j in range(KV_HEADS) for g in range(Q_GROUP)]
    for j in range(KV_HEADS):
        for g in range(Q_GROUP):
            h = j * Q_GROUP + g
            st = scores[h]
            m_prev = m_scr[h]
            m_new = jnp.maximum(m_prev, jnp.max(st, axis=0, keepdims=True))
            alpha = jnp.exp(m_prev - m_new)
            p = jnp.exp(st - m_new)
            acc_scr[h] = alpha * acc_scr[h] + _dot(vt_ref[g % 2, j], p.astype(BF16))
            m_scr[h] = m_new

    @pl.when(kc == pl.num_programs(2) - 1)
    def _():
        lane = lax.broadcasted_iota(jnp.int32, (tq, LANES), 1)
        for c in range(KV_HEADS * Q_GROUP // 2):
            even, odd = acc_scr[2 * c], acc_scr[2 * c + 1]
            even = (even / even[HEAD_DIM:HEAD_DIM + 1]).T
            odd = (odd / odd[0:1]).T
            o_ref[:, c * LANES:(c + 1) * LANES] = jnp.where(lane < HEAD_DIM, even, odd).astype(BF16)


def _att_dense(qt, k_all, vt_all, *, nctx, dec_batch, dec_seq):
    tq = ATT_B_TQ
    nblk = dec_seq // tq
    base = nctx // tq
    nkeys = k_all.shape[2]
    tk = next(t for t in ATT_B_TK if nkeys % t == 0)
    nheads = KV_HEADS * Q_GROUP
    return pl.pallas_call(
        _att_dense_kernel,
        grid=(dec_batch, nblk, nkeys // tk),
        in_specs=[pl.BlockSpec((KV_HEADS, Q_GROUP, LANES, tq),
                               lambda b, i, k: (0, 0, 0, base + b * nblk + i)),
                  pl.BlockSpec((None, KV_HEADS, tk, LANES), lambda b, i, k: (b, 0, k, 0)),
                  pl.BlockSpec((None, 2, KV_HEADS, LANES, tk), lambda b, i, k: (b, 0, 0, 0, k))],
        out_specs=pl.BlockSpec((tq, Q_WIDTH), lambda b, i, k: (b * nblk + i, 0)),
        out_shape=jax.ShapeDtypeStruct((dec_batch * dec_seq, Q_WIDTH), BF16),
        scratch_shapes=[pltpu.VMEM((nheads, 1, tq), F32), pltpu.VMEM((nheads, LANES, tq), F32)],
        compiler_params=_cparams(("parallel", "parallel", "arbitrary")),
        name="att_dense",
    )(qt, k_all, vt_all)


def _lru_kernel(xf_ref, xfp_ref, xfn_ref, xb_ref, xbp_ref, xbn_ref, cw_ref, cb_ref,
                wa_hi_ref, wa_lo_ref, wx_hi_ref, wx_lo_ref, ba_ref, bx_ref, lam_ref,
                h0f_ref, h0b_ref, hf_ref, hb_ref,
                af_scr, bf_scr, ab_scr, bb_scr, sf_scr, sb_scr, *, nctx_blk, seq_blk):
    i = pl.program_id(0)
    nchunk = pl.num_programs(0)
    ts = xf_ref.shape[0]
    row = lax.broadcasted_iota(jnp.int32, (ts, LRU_WIDTH), 0)
    tile_row = lax.broadcasted_iota(jnp.int32, (SUBLANES, LRU_WIDTH), 0)

    def chunk_info(c):
        lat = c >= nctx_blk
        pos = (c - nctx_blk) % seq_blk
        starts = jnp.logical_or(jnp.logical_not(lat), pos == 0)
        ends = jnp.logical_or(jnp.logical_not(lat), pos == seq_blk - 1)
        group = jnp.where(lat, 1 + (c - nctx_blk) // seq_blk, 0)
        return starts, ends, group

    def gates(x_ref, xp_ref, xn_ref, starts, ends, d, a_scr, b_scr):
        x = x_ref[...]
        prev = jnp.where(starts, 0.0, xp_ref[SUBLANES - 1:SUBLANES, :])
        nxt0 = jnp.where(ends, 0.0, xn_ref[0:1, :])
        nxt1 = jnp.where(ends, 0.0, xn_ref[1:2, :])
        x_m1 = jnp.where(row == 0, prev, pltpu.roll(x, 1, 0))
        x_p1 = jnp.where(row == ts - 1, nxt0, pltpu.roll(x, ts - 1, 0))
        x_p2 = jnp.where(row == ts - 2, nxt0, jnp.where(row == ts - 1, nxt1, pltpu.roll(x, ts - 2, 0)))
        xc = (cw_ref[0:1, :] * x_m1 + cw_ref[1:2, :] * x + cw_ref[2:3, :] * x_p1
              + cw_ref[3:4, :] * x_p2 + cb_ref[...])
        xc_hi, xc_lo = _split(xc)

        def blockdiag(hi_ref, lo_ref):
            return _dot(xc_hi, hi_ref[d]) + _dot(xc_lo, hi_ref[d]) + _dot(xc_hi, lo_ref[d])

        r = _sigmoid(blockdiag(wa_hi_ref, wa_lo_ref) + ba_ref[d])
        g = _sigmoid(blockdiag(wx_hi_ref, wx_lo_ref) + bx_ref[d])
        neg_lam = -lam_ref[d]
        softplus = jnp.maximum(neg_lam, 0.0) + jnp.log1p(jnp.exp(-jnp.abs(neg_lam)))
        log_a = -LRU_C * r * softplus
        a = jnp.exp(log_a)
        a_scr[...] = a
        b_scr[...] = jnp.sqrt(1.0 - a * a) * (g * xc)

    f_starts, f_ends, f_group = chunk_info(i)
    cb = nchunk - 1 - i
    b_starts, b_ends, b_group = chunk_info(cb)
    gates(xf_ref, xfp_ref, xfn_ref, f_starts, f_ends, 0, af_scr, bf_scr)
    gates(xb_ref, xbp_ref, xbn_ref, b_starts, b_ends, 1, ab_scr, bb_scr)

    @pl.when(f_starts)
    def _():
        sf_scr[...] = jnp.broadcast_to(h0f_ref[pl.ds(f_group, 1), :], sf_scr.shape)

    @pl.when(b_ends)
    def _():
        sb_scr[...] = jnp.broadcast_to(h0b_ref[pl.ds(b_group, 1), :], sb_scr.shape)

    ntile = ts // SUBLANES

    def tile_step(k, carry):
        hf, hb = carry
        base_f = pl.multiple_of(k * SUBLANES, SUBLANES)
        base_b = pl.multiple_of((ntile - 1 - k) * SUBLANES, SUBLANES)
        a_f = af_scr[pl.ds(base_f, SUBLANES), :]
        b_f = bf_scr[pl.ds(base_f, SUBLANES), :]
        a_b = ab_scr[pl.ds(base_b, SUBLANES), :]
        b_b = bb_scr[pl.ds(base_b, SUBLANES), :]
        out_f = jnp.zeros((SUBLANES, LRU_WIDTH), F32)
        out_b = jnp.zeros((SUBLANES, LRU_WIDTH), F32)
        for r in range(SUBLANES):
            rb = SUBLANES - 1 - r
            hf = a_f[r:r + 1, :] * hf + b_f[r:r + 1, :]
            hb = a_b[rb:rb + 1, :] * hb + b_b[rb:rb + 1, :]
            out_f = jnp.where(tile_row == r, hf, out_f)
            out_b = jnp.where(tile_row == rb, hb, out_b)
        hf_ref[pl.ds(base_f, SUBLANES), :] = out_f
        hb_ref[pl.ds(base_b, SUBLANES), :] = out_b
        return hf, hb

    hf, hb = lax.fori_loop(0, ntile, tile_step, (sf_scr[0:1, :], sb_scr[0:1, :]))
    sf_scr[...] = jnp.broadcast_to(hf, sf_scr.shape)
    sb_scr[...] = jnp.broadcast_to(hb, sb_scr.shape)


def _lru(z1, conv_w, conv_b, wa_hi, wa_lo, wx_hi, wx_lo, ba, bx, lam, h0f, h0b, *, nctx, dec_seq):
    n = z1.shape[0]
    ts = TOKEN_TILE
    nchunk = n // ts
    xcol = COL_XR // LRU_WIDTH
    tiles = ts // SUBLANES
    last_tile = n // SUBLANES - 1

    def cur(rev):
        return lambda i: ((nchunk - 1 - i) if rev else i, xcol)

    def prev(rev):
        return lambda i: (jnp.maximum(((nchunk - 1 - i) if rev else i) * tiles - 1, 0), xcol)

    def nxt(rev):
        return lambda i: (jnp.minimum((((nchunk - 1 - i) if rev else i) + 1) * tiles, last_tile), xcol)

    x_specs = []
    for rev in (False, True):
        x_specs += [pl.BlockSpec((ts, LRU_WIDTH), cur(rev)),
                    pl.BlockSpec((SUBLANES, LRU_WIDTH), prev(rev)),
                    pl.BlockSpec((SUBLANES, LRU_WIDTH), nxt(rev))]
    full = lambda shape: pl.BlockSpec(shape, lambda i: (0,) * len(shape))
    w_spec = full((2, LRU_WIDTH, LRU_WIDTH))
    v_spec = full((2, 1, LRU_WIDTH))
    out_shape = jax.ShapeDtypeStruct((n, LRU_WIDTH), F32)
    scr = pltpu.VMEM((ts, LRU_WIDTH), F32)
    state = pltpu.VMEM((SUBLANES, LRU_WIDTH), F32)
    return pl.pallas_call(
        functools.partial(_lru_kernel, nctx_blk=nctx // ts, seq_blk=dec_seq // ts),
        grid=(nchunk,),
        in_specs=x_specs + [full((4, LRU_WIDTH)), full((1, LRU_WIDTH)),
                            w_spec, w_spec, w_spec, w_spec, v_spec, v_spec, v_spec,
                            full((SUBLANES, LRU_WIDTH)), full((SUBLANES, LRU_WIDTH))],
        out_specs=[pl.BlockSpec((ts, LRU_WIDTH), lambda i: (i, 0)),
                   pl.BlockSpec((ts, LRU_WIDTH), lambda i: (nchunk - 1 - i, 0))],
        out_shape=[out_shape, out_shape],
        scratch_shapes=[scr, scr, scr, scr, state, state],
        compiler_params=_cparams(("arbitrary",)),
        name="lru",
    )(z1, z1, z1, z1, z1, z1, conv_w, conv_b, wa_hi, wa_lo, wx_hi, wx_lo, ba, bx, lam, h0f, h0b)


def _merge_kernel(x_ref, mod_ref, oa_ctx_ref, oa_lat_ref, ob_ctx_ref, ob_lat_ref, hf_ref, hb_ref, z2_ref,
                  woa_ref, wob_ref, woc_ref, wout_ref, o_ref, *, nctx_blk):
    d = x_ref.shape[1]
    is_ctx = pl.program_id(0) < nctx_blk
    oa = jnp.where(is_ctx, oa_ctx_ref[...], oa_lat_ref[...])
    ob = jnp.where(is_ctx, ob_ctx_ref[...], ob_lat_ref[...])
    oc = (hf_ref[...] + hb_ref[...]) * _gelu(z2_ref[:, 0:LRU_WIDTH])
    ga = z2_ref[:, LRU_WIDTH:LRU_WIDTH + d]
    gb = z2_ref[:, LRU_WIDTH + d:LRU_WIDTH + 2 * d]
    gc = z2_ref[:, LRU_WIDTH + 2 * d:LRU_WIDTH + 3 * d]
    merged = (_sigmoid(ga) * _dot(oa, woa_ref[...])
              + _sigmoid(gb) * _dot(ob, wob_ref[...])
              + _sigmoid(gc) * _dot(oc.astype(BF16), woc_ref[...]))
    out = _dot(merged.astype(BF16), wout_ref[...])
    gate = mod_ref[0][:, 2 * d:3 * d]
    o_ref[...] = x_ref[...] + gate * out


def _merge(x, mod3, oa_ctx, oa_lat, ob_ctx, ob_lat, hf, hb, z2, wo_a, wo_b, wo_c, w_out, *, nctx, dec_seq):
    n, d = x.shape
    tm = TOKEN_TILE
    nctx_blk = nctx // tm
    row = lambda w: pl.BlockSpec((tm, w), lambda i: (i, 0))
    ctx_row = pl.BlockSpec((tm, Q_WIDTH), lambda i: (jnp.minimum(i, nctx_blk - 1), 0))
    lat_row = pl.BlockSpec((tm, Q_WIDTH), lambda i: (jnp.maximum(i - nctx_blk, 0), 0))
    full = lambda a: pl.BlockSpec(a.shape, lambda i: (0, 0))
    return pl.pallas_call(
        functools.partial(_merge_kernel, nctx_blk=nctx_blk),
        grid=(n // tm,),
        in_specs=[row(d),
                  pl.BlockSpec((1, 1, mod3.shape[2]),
                               lambda i: (_group_of_block(i, tm, nctx, dec_seq), 0, 0)),
                  ctx_row, lat_row, ctx_row, lat_row, row(LRU_WIDTH), row(LRU_WIDTH), row(Z2_WIDTH),
                  full(wo_a), full(wo_b), full(wo_c), full(w_out)],
        out_specs=row(d),
        out_shape=jax.ShapeDtypeStruct((n, d), F32),
        compiler_params=_cparams(("parallel",)),
        name="merge",
    )(x, mod3, oa_ctx, oa_lat, ob_ctx, ob_lat, hf, hb, z2, wo_a, wo_b, wo_c, w_out)


ORDER_SENTINEL = 1e9


def _candidate_groups():
    k, s = PEER_TOPK, SUBLANES
    split = 4
    groups, covered = [], set()
    for a in range(split):
        for b0 in range(0, k // (a + 1), s):
            groups.append((True, a, b0, 0))
            covered |= {(a, b) for b in range(b0, b0 + s)}
    for b in range(k // (split + 1)):
        for a0 in range(0, k // (b + 1), s):
            if a0 + s > split:
                groups.append((False, a0, b, max(split, a0)))
                covered |= {(a, b) for a in range(max(split, a0), a0 + s)}
    needed = {(a, b) for a in range(k) for b in range(k) if (a + 1) * (b + 1) <= k}
    assert needed <= covered and len(covered) == sum(
        s if f else a0 + s - amin for f, a0, _, amin in groups)
    return groups


class _TopRows:
    def __init__(self, s, extras=(), order=None):
        self.s = s
        self.rows = lax.broadcasted_iota(jnp.int32, s.shape, 0).astype(F32) if order is None else order
        self.extras = extras
        self.vals, self.idxs, self.picked = [], [], [[] for _ in extras]

    def round(self, anchor=None):
        m = jnp.max(self.s, axis=0, keepdims=True)
        if anchor is not None:
            m = jnp.maximum(m, anchor)
        pos = jnp.min(jnp.where(self.s == m, self.rows, ORDER_SENTINEL), axis=0, keepdims=True)
        sel = self.rows == pos
        self.vals.append(m)
        self.idxs.append(pos)
        for dst, ex in zip(self.picked, self.extras):
            dst.append(jnp.max(jnp.where(sel, ex, -1.0), axis=0, keepdims=True))
        self.s = jnp.where(sel, -jnp.inf, self.s)

    def run(self, k):
        for _ in range(k):
            self.round()
        return self

    def result(self):
        cat = lambda xs: jnp.concatenate(xs, axis=0)
        return cat(self.vals), cat(self.idxs), [cat(p) for p in self.picked]


def _subkey_tops(selectors):
    return [a for sel in selectors for a in sel.result()[:2]]


def _pair_gates(selector):
    best, _, (e1, e2) = selector.result()
    ex = jnp.exp(best - best[0:1])
    return ex / jnp.sum(ex, axis=0, keepdims=True), e1, e2


def _peer_topk_kernel(q_ref, khi_ref, klo_ref, g_ref, i1_ref, i2_ref):
    for t0 in range(0, q_ref.shape[0], LANES):
        subs = [_TopRows(s).run(PEER_TOPK) for s in _peer_scores(q_ref, khi_ref, klo_ref, t0)]
        picked = _pair_gates(_peer_pair_rows(*_subkey_tops(subs)).run(PEER_TOPK))
        g_ref[0, :, t0:t0 + LANES], i1_ref[0, :, t0:t0 + LANES], i2_ref[0, :, t0:t0 + LANES] = picked


def _peer_scores(q_ref, khi_ref, klo_ref, t0):
    scores = []
    for p in range(2):
        q_hi, q_lo = _split(q_ref[t0:t0 + LANES, p * PEER_DKEY:(p + 1) * PEER_DKEY])
        k_hi, k_lo = khi_ref[0, p], klo_ref[0, p]
        scores.append(_dot_nt(k_hi, q_hi) + _dot_nt(k_lo, q_hi) + _dot_nt(k_hi, q_lo))
    return scores


def _peer_pair_rows(vals0, idx0, vals1, idx1):
    top_s, top_i = (vals0, vals1), (idx0, idx1)
    tb = top_s[0].shape[1]
    gshape = (SUBLANES, tb)
    sub_row = lax.broadcasted_iota(jnp.int32, gshape, 0).astype(F32)
    cand_s, cand_1, cand_2, order = [], [], [], []
    for fixed_first, a, b, a_min in _candidate_groups():
        if fixed_first:
            bcast = lambda x: jnp.broadcast_to(x[a:a + 1], gshape)
            cand_s.append(bcast(top_s[0]) + top_s[1][b:b + SUBLANES])
            cand_1.append(bcast(top_i[0]))
            cand_2.append(top_i[1][b:b + SUBLANES])
            order.append(float(a * PEER_TOPK + b) + sub_row)
        else:
            bcast = lambda x: jnp.broadcast_to(x[b:b + 1], gshape)
            live = sub_row >= float(a_min - a)
            cand_s.append(jnp.where(live, top_s[0][a:a + SUBLANES] + bcast(top_s[1]), -jnp.inf))
            cand_1.append(top_i[0][a:a + SUBLANES])
            cand_2.append(bcast(top_i[1]))
            order.append(jnp.where(live, float(a * PEER_TOPK + b) + float(PEER_TOPK) * sub_row,
                                   ORDER_SENTINEL))
    cat = lambda xs: jnp.concatenate(xs, axis=0)
    return _TopRows(cat(cand_s), (cat(cand_1), cat(cand_2)), cat(order))


def _peer_topk(q, keys_hi, keys_lo):
    n = q.shape[0]
    tb = PEER_TB
    spec = pl.BlockSpec((1, PEER_TOPK, tb), lambda i, h: (h, 0, i))
    shape = jax.ShapeDtypeStruct((PEER_HEADS, PEER_TOPK, n), F32)
    k_spec = pl.BlockSpec((1, 2, PEER_NKEYS, PEER_DKEY), lambda i, h: (h, 0, 0, 0))
    return pl.pallas_call(
        _peer_topk_kernel,
        grid=(n // tb, PEER_HEADS),
        in_specs=[pl.BlockSpec((tb, 2 * PEER_DKEY), lambda i, h: (i, h)), k_spec, k_spec],
        out_specs=[spec, spec, spec],
        out_shape=[shape, shape, shape],
        compiler_params=_cparams(("parallel", "parallel")),
        name="peer_topk",
    )(q, keys_hi, keys_lo)


def _peer_expert_kernel(x_ref, mod_ref, gn_ref, first_g_ref, first_i1_ref, first_i2_ref,
                        qn_ref, khi_ref, klo_ref, ut_ref, v_ref, o_ref,
                        h_scr, w_scr, p_scr, acc_scr, sub_scr, pair_scr, tok_scr):
    i = pl.program_id(0)
    c = pl.program_id(1)
    tb, d = x_ref.shape
    ec = ut_ref.shape[1]
    sub = ec // PEER_NKEYS
    npair = PEER_HEADS * PEER_TOPK
    g_ref, i1_ref, i2_ref = tok_scr.at[0], tok_scr.at[1], tok_scr.at[2]

    tiles = range(0, tb, LANES)

    def pair_selectors():
        return [_peer_pair_rows(*[sub_scr[k, :, t0:t0 + LANES] for k in range(4)]) for t0 in tiles]

    def store_pairs(head, selectors):
        rows = pl.ds(pl.multiple_of(head * PEER_TOPK, PEER_TOPK), PEER_TOPK)
        for t0, sel in zip(tiles, selectors):
            for k, a in enumerate(_pair_gates(sel)):
                pair_scr[k, rows, t0:t0 + LANES] = a

    @pl.when(jnp.logical_and(c == 0, i == 0))
    def _():
        sub_scr[...] = jnp.zeros(sub_scr.shape, F32)
        for k, ref in enumerate((first_g_ref, first_i1_ref, first_i2_ref)):
            tok_scr[k] = ref[...].reshape(npair, tb).T

    @pl.when(jnp.logical_and(c == 0, i > 0))
    def _():
        store_pairs(PEER_HEADS - 1, [sel.run(PEER_TOPK) for sel in pair_selectors()])
        for k in range(3):
            tok_scr[k] = pair_scr[k].T

    @pl.when(c == 0)
    def _():
        x = x_ref[...]
        y = x * lax.rsqrt(jnp.mean(x * x, axis=-1, keepdims=True) + EPS) * gn_ref[...]
        m = mod_ref[0]
        h_scr[...] = (y * (1.0 + m[:, 4 * d:5 * d]) + m[:, 3 * d:4 * d]).astype(BF16)
        acc_scr[...] = jnp.zeros(acc_scr.shape, F32)
        key_row = lax.broadcasted_iota(jnp.int32, (PEER_NKEYS, npair), 0).astype(F32)

        def token_group(tg, carry):
            for u in range(PEER_TOKEN_UNROLL):
                t = tg * PEER_TOKEN_UNROLL + u
                gate = g_ref[pl.ds(t, 1), :]
                ga = jnp.where(key_row == i1_ref[pl.ds(t, 1), :], gate, 0.0).astype(BF16)
                ob = jnp.where(key_row == i2_ref[pl.ds(t, 1), :], 1.0, 0.0).astype(BF16)
                w_scr[pl.ds(pl.multiple_of(t * PEER_W_PITCH, SUBLANES), PEER_NKEYS), :] = _dot_nt(ga, ob)
            return carry

        lax.fori_loop(0, tb // PEER_TOKEN_UNROLL, token_group, 0)

    pair_sel = pair_selectors()
    sub_sel = [[_TopRows(s) for s in _peer_scores(qn_ref, khi_ref, klo_ref, t0)] for t0 in tiles]
    rounds = []
    for _ in range(PEER_TOPK):
        rounds.append([sel for two in sub_sel for sel in two])
        rounds.append(pair_sel)
    nblock = ec // PEER_BLOCK
    per_block = -(-len(rounds) // nblock)

    def advance(h_b):
        anchor = jnp.minimum(h_b[0:1, 0:LANES], -jnp.inf)
        for n, sels in enumerate(rounds[:per_block]):
            for sel in sels:
                sel.round(anchor if n == 0 else None)
        del rounds[:per_block]

    def activate(b, h_b):
        for s in range(b * PEER_BLOCK // PEER_NKEYS, (b + 1) * PEER_BLOCK // PEER_NKEYS):
            w = w_scr[pl.ds(c * sub + s, tb, stride=PEER_W_PITCH), :]
            lo = s * PEER_NKEYS - b * PEER_BLOCK
            x = h_b[:, lo:lo + PEER_NKEYS]
            t = jnp.tanh(_gelu_inner(x).astype(BF16))
            p_scr[:, s * PEER_NKEYS:(s + 1) * PEER_NKEYS] = (0.5 * w).astype(BF16) * (x.astype(BF16) * (1.0 + t))
        return _dot(p_scr[:, b * PEER_BLOCK:(b + 1) * PEER_BLOCK], v_ref[b * PEER_BLOCK:(b + 1) * PEER_BLOCK, :])

    acc = acc_scr[...]
    h_prev = None
    for b in range(nblock):
        h_b = _dot(h_scr[...], ut_ref[:, b * PEER_BLOCK:(b + 1) * PEER_BLOCK])
        advance(h_b)
        if h_prev is not None:
            acc = acc + activate(b - 1, h_prev)
        h_prev = h_b
    acc_scr[...] = acc + activate(nblock - 1, h_prev)
    assert not rounds

    store_pairs(jnp.where(c == 0, PEER_HEADS - 1, c - 1), pair_sel)
    for t0, two in zip(tiles, sub_sel):
        for k, a in enumerate(_subkey_tops(two)):
            sub_scr[k, :, t0:t0 + LANES] = a

    @pl.when(c == pl.num_programs(1) - 1)
    def _():
        o_ref[...] = x_ref[...] + mod_ref[0][:, 5 * d:6 * d] * acc_scr[...]


def _peer_expert(x, mod3, gain, q, keys_hi, keys_lo, ut, v, *, nctx, dec_seq):
    n, d = x.shape
    tb, ec = PEER_TB, PEER_EC
    nexp = v.shape[0]
    ntile = n // tb
    npair = PEER_HEADS * PEER_TOPK
    assert nexp // ec == PEER_HEADS
    first = _peer_topk(q[:tb], keys_hi, keys_lo)
    row = lambda w: pl.BlockSpec((tb, w), lambda i, c: (i, 0))
    first_spec = pl.BlockSpec((PEER_HEADS, PEER_TOPK, tb), lambda i, c: (0, 0, 0))
    k_spec = pl.BlockSpec((1, 2, PEER_NKEYS, PEER_DKEY), lambda i, c: (c, 0, 0, 0))
    return pl.pallas_call(
        _peer_expert_kernel,
        grid=(ntile, nexp // ec),
        in_specs=[row(d),
                  pl.BlockSpec((1, 1, mod3.shape[2]),
                               lambda i, c: (_group_of_block(i, tb, nctx, dec_seq), 0, 0)),
                  pl.BlockSpec((1, d), lambda i, c: (0, 0)),
                  first_spec, first_spec, first_spec,
                  pl.BlockSpec((tb, 2 * PEER_DKEY), lambda i, c: (jnp.minimum(i + 1, ntile - 1), c)),
                  k_spec, k_spec,
                  pl.BlockSpec((d, ec), lambda i, c: (0, c)),
                  pl.BlockSpec((ec, d), lambda i, c: (c, 0))],
        out_specs=row(d),
        out_shape=jax.ShapeDtypeStruct((n, d), F32),
        scratch_shapes=[pltpu.VMEM((tb, d), BF16),
                        pltpu.VMEM((tb * PEER_W_PITCH, PEER_NKEYS), F32),
                        pltpu.VMEM((tb, ec), BF16),
                        pltpu.VMEM((tb, d), F32),
                        pltpu.VMEM((4, PEER_TOPK, tb), F32),
                        pltpu.VMEM((3, npair, tb), F32),
                        pltpu.VMEM((3, tb, npair), F32)],
        compiler_params=_cparams(("arbitrary", "arbitrary")),
        name="peer_expert",
    )(x, mod3, gain.reshape(1, d), *first, q, keys_hi, keys_lo, ut, v)


def _final_norm_kernel(x_ref, g_ref, octx_ref, olat_ref, *, nctx_blk):
    i = pl.program_id(0)
    x = x_ref[...]
    y = x * lax.rsqrt(jnp.mean(x * x, axis=-1, keepdims=True) + EPS) * g_ref[...]

    @pl.when(i < nctx_blk)
    def _():
        octx_ref[...] = y

    @pl.when(i >= nctx_blk)
    def _():
        olat_ref[...] = y


def _final_norm(x, gain, *, nctx):
    n, d = x.shape
    tm = 512
    nctx_blk = nctx // tm
    return pl.pallas_call(
        functools.partial(_final_norm_kernel, nctx_blk=nctx_blk),
        grid=(n // tm,),
        in_specs=[pl.BlockSpec((tm, d), lambda i: (i, 0)), pl.BlockSpec((1, d), lambda i: (0, 0))],
        out_specs=[pl.BlockSpec((tm, d), lambda i: (jnp.minimum(i, nctx_blk - 1), 0)),
                   pl.BlockSpec((tm, d), lambda i: (jnp.maximum(i - nctx_blk, 0), 0))],
        out_shape=[jax.ShapeDtypeStruct((nctx, d), F32), jax.ShapeDtypeStruct((n - nctx, d), F32)],
        compiler_params=_cparams(("arbitrary",)),
        name="final_norm",
    )(x, gain.reshape(1, d))


def _rope_tables(dec_seq, tile):
    t = jnp.arange(dec_seq)
    n_freq = HEAD_DIM // 4
    inv = ROPE_BASE ** (-jnp.arange(n_freq, dtype=F32) / n_freq)
    ang = jnp.concatenate([(t // GRID_W).astype(F32)[:, None] * inv,
                           (t % GRID_W).astype(F32)[:, None] * inv], axis=-1)
    cos, sin = jnp.cos(ang), jnp.sin(ang)
    reps = LANES // HEAD_DIM
    cos_t = jnp.tile(jnp.concatenate([cos, cos], axis=-1), (1, reps))
    sin_t = jnp.tile(jnp.concatenate([-sin, sin], axis=-1), (1, reps))
    cos_t = jnp.concatenate([jnp.ones((tile, LANES), F32), cos_t], axis=0)
    sin_t = jnp.concatenate([jnp.zeros((tile, LANES), F32), sin_t], axis=0)
    return cos_t, sin_t


def _dup_cache(cache):
    c = jnp.transpose(cache, (0, 2, 1, 3))
    return jnp.concatenate([c, c], axis=-1).astype(BF16)


def _dense_blockdiag(w):
    dirs, nb, bw, _ = w.shape
    eye = jnp.eye(nb, dtype=w.dtype)
    return jnp.einsum('dncf,nm->dncmf', w, eye).reshape(dirs, nb * bw, nb * bw)


def _pad_rows(a, rows):
    return jnp.concatenate([a, jnp.zeros((rows - a.shape[0],) + a.shape[1:], a.dtype)], axis=0)


def kernel(x_prompt, x_sample, c, cache_wa_k, cache_wa_v, cache_ax_k, cache_ax_v, state_lru_fwd,
           state_lru_bwd, c_ctx, w_ada, b_ada, g_norm1, w_in, wa_sink, ax_q_gain, ax_k_gain, conv_w,
           conv_b, lru_wa, lru_ba, lru_wx, lru_bx, lru_lambda, wo_a, wo_b, wo_c, w_out, g_norm2,
           peer_wq, peer_keys, peer_u, peer_v, g_final):
    batch, seq, d = x_prompt.shape
    dec_batch, dec_seq, _ = x_sample.shape
    depth = w_in.shape[0]
    nctx = batch * seq
    nlat = dec_batch * dec_seq
    assert seq == TOKEN_TILE and dec_seq % TOKEN_TILE == 0 and 1 + dec_batch <= SUBLANES
    assert w_in.shape[2] == Z1_WIDTH + Z2_WIDTH and d == 1024
    sizes = dict(nctx=nctx, dec_seq=dec_seq)

    x = jnp.concatenate([x_prompt.reshape(nctx, d), x_sample.reshape(nlat, d)], axis=0)
    cvec = _pad_rows(jnp.concatenate([c_ctx[None, :], c], axis=0), SUBLANES)
    mods = _ada(cvec, w_ada, b_ada)

    cos_tab, sin_tab = _rope_tables(dec_seq, TOKEN_TILE)
    head_mean = jnp.kron(jnp.eye(LANES // HEAD_DIM, dtype=F32),
                         jnp.full((HEAD_DIM, HEAD_DIM), 1.0 / HEAD_DIM, F32)).astype(BF16)
    reps = LANES // HEAD_DIM

    new_wa_k, new_wa_v, new_ax_k, new_ax_v, new_hf, new_hb = [], [], [], [], [], []
    for l in range(depth):
        mod3 = mods[l].reshape(SUBLANES, 1, 6 * d)
        w_in_bf = w_in[l].astype(BF16)
        z1 = _modmm(x, mod3, g_norm1[l], w_in_bf[:, :Z1_WIDTH], None, shift_col=0, tn=Z1_WIDTH, **sizes)
        z2 = _modmm(x, mod3, g_norm1[l], w_in_bf[:, Z1_WIDTH:], None, shift_col=0, tn=Z2_WIDTH, **sizes)

        gq = jnp.tile(ax_q_gain[l], reps).reshape(1, LANES)
        gk = jnp.tile(ax_k_gain[l], reps).reshape(1, LANES)
        qsa, kka, vva, qsb, kkb, vvb, kbn, qta, qtb = _prep(z1, cos_tab, sin_tab, gq, gk, head_mean, **sizes)

        oa_ctx, ob_ctx = _att_ctx(wa_sink[l], qsa, kka, vva, qsb, kkb, vvb, nseq=batch, seq=seq)
        oa_lat = _att_win(wa_sink[l], qta, kka, jnp.swapaxes(vva, 1, 2), _dup_cache(cache_wa_k[:, l]),
                          jnp.swapaxes(_dup_cache(cache_wa_v[:, l]), 2, 3),
                          nctx=nctx, dec_batch=dec_batch, dec_seq=dec_seq)

        def with_cache(cache, cur):
            lat = cur[:, nctx:].reshape(KV_HEADS, dec_batch, dec_seq, LANES)
            return jnp.concatenate([_dup_cache(cache), jnp.transpose(lat, (1, 0, 2, 3))], axis=2)

        vt = jnp.swapaxes(with_cache(cache_ax_v[:, l], vvb), 2, 3)
        vt2 = jnp.stack([vt.at[:, :, HEAD_DIM].set(1.0), vt.at[:, :, 0].set(1.0)], axis=1)
        ob_lat = _att_dense(qtb, with_cache(cache_ax_k[:, l], kkb), vt2,
                            nctx=nctx, dec_batch=dec_batch, dec_seq=dec_seq)

        wa_hi, wa_lo = _split(_dense_blockdiag(lru_wa[l]))
        wx_hi, wx_lo = _split(_dense_blockdiag(lru_wx[l]))
        vec = lambda a: a.reshape(2, 1, LRU_WIDTH)
        h0f = _pad_rows(jnp.concatenate([jnp.zeros((1, LRU_WIDTH), F32), state_lru_fwd[:, l]], axis=0), SUBLANES)
        h0b = _pad_rows(jnp.concatenate([jnp.zeros((1, LRU_WIDTH), F32), state_lru_bwd[:, l]], axis=0), SUBLANES)
        hf, hb = _lru(z1, conv_w[l], conv_b[l].reshape(1, LRU_WIDTH), wa_hi, wa_lo, wx_hi, wx_lo,
                      vec(lru_ba[l]), vec(lru_bx[l]), vec(lru_lambda[l]), h0f, h0b, **sizes)

        x = _merge(x, mod3, oa_ctx, oa_lat, ob_ctx, ob_lat, hf, hb, z2, wo_a[l].astype(BF16),
                   wo_b[l].astype(BF16), wo_c[l].astype(BF16), w_out[l].astype(BF16), **sizes)

        wq_hi, wq_lo = _split(peer_wq[l])
        q = _modmm(x, mod3, g_norm2[l], wq_hi, wq_lo, shift_col=3 * d, tn=wq_hi.shape[1], **sizes)
        keys_hi, keys_lo = _split(peer_keys[l])
        x = _peer_expert(x, mod3, g_norm2[l], q, keys_hi, keys_lo,
                         peer_u[l].T.astype(BF16), peer_v[l].astype(BF16), **sizes)

        ctx4 = lambda a: a[:nctx].reshape(batch, seq, KV_HEADS, HEAD_DIM)
        new_wa_k.append(ctx4(z1[:, COL_KA:COL_KA + KV_WIDTH]))
        new_wa_v.append(ctx4(z1[:, COL_VA:COL_VA + KV_WIDTH]))
        new_ax_k.append(ctx4(kbn))
        new_ax_v.append(ctx4(z1[:, COL_VB:COL_VB + KV_WIDTH]))
        new_hf.append(hf[:nctx].reshape(batch, seq, LRU_WIDTH)[:, -1])
        new_hb.append(hb[:nctx].reshape(batch, seq, LRU_WIDTH)[:, 0])

    y_ctx, y_lat = _final_norm(x, g_final, nctx=nctx)
    stack = lambda xs: jnp.stack(xs, axis=1)
    return (y_ctx.reshape(batch, seq, d), y_lat.reshape(dec_batch, dec_seq, d),
            stack(new_wa_k), stack(new_wa_v), stack(new_ax_k), stack(new_ax_v),
            stack(new_hf), stack(new_hb))
```

```python
import functools

import jax
import jax.numpy as jnp
from jax import lax
from jax.experimental import pallas as pl
from jax.experimental.pallas import tpu as pltpu

F32 = jnp.float32
BF16 = jnp.bfloat16

HEAD_DIM = 64
KV_HEADS = 2
Q_GROUP = 4
Q_WIDTH = KV_HEADS * Q_GROUP * HEAD_DIM
KV_WIDTH = KV_HEADS * HEAD_DIM
WINDOW = 128
GRID_W = 64
LRU_WIDTH = 512
LRU_BLOCKS = 8
LRU_C = 8.0
PEER_HEADS = 8
PEER_NKEYS = 128
PEER_DKEY = 128
PEER_TOPK = 16
ROPE_BASE = 10000.0
EPS = 1e-6
NEG = -1e30
LANES = 128
SUBLANES = 8
VMEM_LIMIT = 56 * 1024 * 1024

Z1_WIDTH = 2048
Z2_WIDTH = 3584
COL_KA, COL_VA, COL_QB, COL_KB, COL_VB, COL_XR = 512, 640, 768, 1280, 1408, 1536

TOKEN_TILE = 256
ATT_A_TQ = 128
ATT_B_TQ = 256
ATT_B_TK = (1536, 1024, 512)
PEER_TB = 256
PEER_EC = 2048
PEER_BLOCK = 256
PEER_TOKEN_UNROLL = 128
PEER_W_PITCH = PEER_NKEYS + SUBLANES


def _cparams(sem):
    return pltpu.CompilerParams(dimension_semantics=sem, vmem_limit_bytes=VMEM_LIMIT)


def _split(x):
    hi = x.astype(BF16)
    lo = (x - hi.astype(F32)).astype(BF16)
    return hi, lo


def _dot(a, b):
    return jnp.dot(a, b, preferred_element_type=F32)


def _dot_nt(a, b):
    return lax.dot_general(a, b, (((1,), (1,)), ((), ())), preferred_element_type=F32)


def _sigmoid(x):
    return 1.0 / (1.0 + jnp.exp(-x))


def _gelu_inner(x):
    return 0.7978845608028654 * (x + 0.044715 * (x * x * x))


def _gelu(x):
    return 0.5 * x * (1.0 + jnp.tanh(_gelu_inner(x)))


def _group_of_block(i, tile, nctx, dec_seq):
    row = i * tile
    return jnp.where(row < nctx, 0, 1 + (row - nctx) // dec_seq)


def _ada_kernel(c_ref, w_ref, b_ref, o_ref):
    c = c_ref[...]
    s = c * _sigmoid(c)
    s_hi, s_lo = _split(s)
    w_hi, w_lo = _split(w_ref[0])
    o_ref[0] = _dot(s_hi, w_hi) + _dot(s_lo, w_hi) + _dot(s_hi, w_lo) + b_ref[0]


def _ada(cvec, w_ada, b_ada):
    depth, d, e = w_ada.shape
    tn = 1536
    return pl.pallas_call(
        _ada_kernel,
        grid=(depth, e // tn),
        in_specs=[
            pl.BlockSpec((SUBLANES, d), lambda l, j: (0, 0)),
            pl.BlockSpec((1, d, tn), lambda l, j: (l, 0, j)),
            pl.BlockSpec((1, 1, tn), lambda l, j: (l, 0, j)),
        ],
        out_specs=pl.BlockSpec((1, SUBLANES, tn), lambda l, j: (l, 0, j)),
        out_shape=jax.ShapeDtypeStruct((depth, SUBLANES, e), F32),
        compiler_params=_cparams(("parallel", "parallel")),
        name="ada",
    )(cvec, w_ada, b_ada.reshape(depth, 1, e))


def _modmm_kernel(x_ref, mod_ref, g_ref, w_ref, *rest, shift_col, three_pass):
    d = x_ref.shape[1]
    if three_pass:
        wlo_ref, o_ref, h_scr, hlo_scr = rest
    else:
        o_ref, h_scr = rest

    @pl.when(pl.program_id(1) == 0)
    def _():
        x = x_ref[...]
        y = x * lax.rsqrt(jnp.mean(x * x, axis=-1, keepdims=True) + EPS) * g_ref[...]
        m = mod_ref[0]
        h = y * (1.0 + m[:, shift_col + d:shift_col + 2 * d]) + m[:, shift_col:shift_col + d]
        hi = h.astype(BF16)
        h_scr[...] = hi
        if three_pass:
            hlo_scr[...] = (h - hi.astype(F32)).astype(BF16)

    acc = _dot(h_scr[...], w_ref[...])
    if three_pass:
        acc = acc + _dot(hlo_scr[...], w_ref[...]) + _dot(h_scr[...], wlo_ref[...])
    o_ref[...] = acc


def _modmm(x, mod3, gain, w, w_lo, *, shift_col, tn, nctx, dec_seq):
    n, d = x.shape
    width = w.shape[1]
    tm = 512
    three_pass = w_lo is not None
    grp = lambda i, j: (_group_of_block(i, tm, nctx, dec_seq), 0, 0)
    in_specs = [
        pl.BlockSpec((tm, d), lambda i, j: (i, 0)),
        pl.BlockSpec((1, 1, mod3.shape[2]), grp),
        pl.BlockSpec((1, d), lambda i, j: (0, 0)),
        pl.BlockSpec((d, tn), lambda i, j: (0, j)),
    ]
    args = [x, mod3, gain.reshape(1, d), w]
    scratch = [pltpu.VMEM((tm, d), BF16)]
    if three_pass:
        in_specs.append(pl.BlockSpec((d, tn), lambda i, j: (0, j)))
        args.append(w_lo)
        scratch.append(pltpu.VMEM((tm, d), BF16))
    return pl.pallas_call(
        functools.partial(_modmm_kernel, shift_col=shift_col, three_pass=three_pass),
        grid=(n // tm, width // tn),
        in_specs=in_specs,
        out_specs=pl.BlockSpec((tm, tn), lambda i, j: (i, j)),
        out_shape=jax.ShapeDtypeStruct((n, width), F32),
        scratch_shapes=scratch,
        compiler_params=_cparams(("parallel", "arbitrary")),
        name="modmm",
    )(*args)


def _prep_kernel(z_ref, cos_ref, sin_ref, gq_ref, gk_ref, m_ref,
                 qsa_ref, kka_ref, vva_ref, qsb_ref, kkb_ref, vvb_ref, kbn_ref, qta_ref, qtb_ref):
    tm = z_ref.shape[0]
    lane = lax.broadcasted_iota(jnp.int32, (tm, LANES), 1)
    low_head = lane < HEAD_DIM
    first_half = (lane & (HEAD_DIM // 2)) == 0
    cos = cos_ref[...]
    sin = sin_ref[...]
    mmat = m_ref[...]
    scale = HEAD_DIM ** -0.5

    def rope(x):
        back = pltpu.roll(x, HEAD_DIM // 2, 1)
        fwd = pltpu.roll(x, LANES - HEAD_DIM // 2, 1)
        return x * cos + jnp.where(first_half, fwd, back) * sin

    def headnorm(x, gain):
        sq_hi, sq_lo = _split(x * x)
        ms = _dot(sq_hi, mmat) + _dot(sq_lo, mmat)
        return x * lax.rsqrt(ms + EPS) * gain

    def store_q(ref, c, q, t_ref=None):
        zero = jnp.zeros_like(q)
        j, g0 = c // 2, 2 * (c % 2)
        for g, qm in ((g0, jnp.where(low_head, q, zero)), (g0 + 1, jnp.where(low_head, zero, q))):
            ref[j, g] = qm.astype(BF16)
            if t_ref is not None:
                t_ref[j, g] = qm.T.astype(BF16)

    def store_dup(ref, x):
        swapped = pltpu.roll(x, HEAD_DIM, 1)
        ref[0] = jnp.where(low_head, x, swapped).astype(BF16)
        ref[1] = jnp.where(low_head, swapped, x).astype(BF16)

    for c in range(Q_WIDTH // LANES):
        store_q(qsa_ref, c, rope(z_ref[:, c * LANES:(c + 1) * LANES]) * scale, qta_ref)
        qb = headnorm(z_ref[:, COL_QB + c * LANES:COL_QB + (c + 1) * LANES], gq_ref[...])
        store_q(qsb_ref, c, rope(qb) * scale, qtb_ref)
    store_dup(kka_ref, rope(z_ref[:, COL_KA:COL_KA + KV_WIDTH]))
    store_dup(vva_ref, z_ref[:, COL_VA:COL_VA + KV_WIDTH])
    kb = rope(headnorm(z_ref[:, COL_KB:COL_KB + KV_WIDTH], gk_ref[...]))
    kbn_ref[...] = kb
    store_dup(kkb_ref, kb)
    store_dup(vvb_ref, z_ref[:, COL_VB:COL_VB + KV_WIDTH])


def _prep(z1, cos_tab, sin_tab, gq, gk, mmat, *, nctx, dec_seq):
    n = z1.shape[0]
    tm = TOKEN_TILE
    nctx_blk = nctx // tm
    seq_blk = dec_seq // tm

    def tab_idx(i):
        return (jnp.where(i < nctx_blk, 0, 1 + (i - nctx_blk) % seq_blk), 0)

    qs_spec = pl.BlockSpec((KV_HEADS, Q_GROUP, tm, LANES), lambda i: (0, 0, i, 0))
    kv_spec = pl.BlockSpec((KV_HEADS, tm, LANES), lambda i: (0, i, 0))
    qs_shape = jax.ShapeDtypeStruct((KV_HEADS, Q_GROUP, n, LANES), BF16)
    qt_spec = pl.BlockSpec((KV_HEADS, Q_GROUP, LANES, tm), lambda i: (0, 0, 0, i))
    qt_shape = jax.ShapeDtypeStruct((KV_HEADS, Q_GROUP, LANES, n), BF16)
    kv_shape = jax.ShapeDtypeStruct((KV_HEADS, n, LANES), BF16)
    return pl.pallas_call(
        _prep_kernel,
        grid=(n // tm,),
        in_specs=[
            pl.BlockSpec((tm, Z1_WIDTH), lambda i: (i, 0)),
            pl.BlockSpec((tm, LANES), tab_idx),
            pl.BlockSpec((tm, LANES), tab_idx),
            pl.BlockSpec((1, LANES), lambda i: (0, 0)),
            pl.BlockSpec((1, LANES), lambda i: (0, 0)),
            pl.BlockSpec((LANES, LANES), lambda i: (0, 0)),
        ],
        out_specs=[qs_spec, kv_spec, kv_spec, qs_spec, kv_spec, kv_spec,
                   pl.BlockSpec((tm, LANES), lambda i: (i, 0)), qt_spec, qt_spec],
        out_shape=[qs_shape, kv_shape, kv_shape, qs_shape, kv_shape, kv_shape,
                   jax.ShapeDtypeStruct((n, LANES), F32), qt_shape, qt_shape],
        compiler_params=_cparams(("parallel",)),
        name="prep",
    )(z1, cos_tab, sin_tab, gq, gk, mmat)


def _merge_heads(o, tq, c):
    lane = lax.broadcasted_iota(jnp.int32, (tq, LANES), 1)
    g = 2 * c
    return jnp.where(lane < HEAD_DIM, o[g * tq:(g + 1) * tq], o[(g + 1) * tq:(g + 2) * tq])


def _sink_column(sink_ref, j, tq):
    return jnp.concatenate(
        [jnp.full((tq, 1), sink_ref[j * Q_GROUP + g], F32) for g in range(Q_GROUP)], axis=0)


def _att_ctx_kernel(sink_ref, qsa_ref, kka_ref, vva_ref, qsb_ref, kkb_ref, vvb_ref, oa_ref, ob_ref):
    tq = qsa_ref.shape[2]
    mixers = ((qsa_ref, kka_ref, vva_ref, oa_ref, True), (qsb_ref, kkb_ref, vvb_ref, ob_ref, False))
    scores = [[_dot_nt(q_ref[j].reshape(Q_GROUP * tq, LANES), k_ref[j]) for j in range(KV_HEADS)]
              for q_ref, k_ref, _, _, _ in mixers]
    for (_, _, v_ref, o_ref, has_sink), mixer_scores in zip(mixers, scores):
        for j in range(KV_HEADS):
            s = mixer_scores[j]
            m = jnp.max(s, axis=-1, keepdims=True)
            if has_sink:
                sk = _sink_column(sink_ref, j, tq)
                m = jnp.maximum(m, sk)
            e = jnp.exp(s - m)
            den = jnp.sum(e, axis=-1, keepdims=True)
            if has_sink:
                den = den + jnp.exp(sk - m)
            o = _dot(e.astype(BF16), v_ref[j]) / den
            for c in range(2):
                col = (2 * j + c) * LANES
                o_ref[:, col:col + LANES] = _merge_heads(o, tq, c).astype(BF16)


def _att_ctx(sink, qsa, kka, vva, qsb, kkb, vvb, *, nseq, seq):
    qs_spec = pl.BlockSpec((KV_HEADS, Q_GROUP, seq, LANES), lambda i: (0, 0, i, 0))
    kv_spec = pl.BlockSpec((KV_HEADS, seq, LANES), lambda i: (0, i, 0))
    o_spec = pl.BlockSpec((seq, Q_WIDTH), lambda i: (i, 0))
    o_shape = jax.ShapeDtypeStruct((nseq * seq, Q_WIDTH), BF16)
    return pl.pallas_call(
        _att_ctx_kernel,
        grid=(nseq,),
        in_specs=[pl.BlockSpec(memory_space=pltpu.SMEM),
                  qs_spec, kv_spec, kv_spec, qs_spec, kv_spec, kv_spec],
        out_specs=[o_spec, o_spec],
        out_shape=[o_shape, o_shape],
        compiler_params=_cparams(("parallel",)),
        name="att_ctx",
    )(sink, qsa, kka, vva, qsb, kkb, vvb)


def _att_win_kernel(sink_ref, qt_ref, kp_ref, kc_ref, kn_ref, vtp_ref, vtc_ref, vtn_ref,
                    kx_ref, vxt_ref, o_ref, *, nblk):
    i = pl.program_id(1)
    tq = qt_ref.shape[3]
    j_idx = lax.broadcasted_iota(jnp.int32, (tq, tq), 0)
    a_idx = lax.broadcasted_iota(jnp.int32, (tq, tq), 1)
    valid_prev = (j_idx >= a_idx) & (i > 0)
    valid_next = (j_idx <= a_idx) & (i < nblk - 1)
    k_refs = (kx_ref, kp_ref, kc_ref, kn_ref)
    vt_refs = (vxt_ref, vtp_ref, vtc_ref, vtn_ref)
    heads = [(j, g) for j in range(KV_HEADS) for g in range(Q_GROUP)]
    raw = [[_dot(k_ref[j], qt_ref[j, g]) for k_ref in k_refs] for j, g in heads]
    outs = []
    for h, (j, g) in enumerate(heads):
        s_x, s_p, s_c, s_n = raw[h]
        blocks = (s_x, jnp.where(valid_prev, s_p, NEG), s_c, jnp.where(valid_next, s_n, NEG))
        sk = sink_ref[h]
        m = jnp.maximum(jnp.max(blocks[0], axis=0, keepdims=True), sk)
        for s in blocks[1:]:
            m = jnp.maximum(m, jnp.max(s, axis=0, keepdims=True))
        den = jnp.exp(sk - m)
        o = jnp.zeros((LANES, tq), F32)
        for s, vt_ref in zip(blocks, vt_refs):
            e = jnp.exp(s - m)
            den = den + jnp.sum(e, axis=0, keepdims=True)
            o = o + _dot(vt_ref[j], e.astype(BF16))
        outs.append((o / den).T)
    lane = lax.broadcasted_iota(jnp.int32, (tq, LANES), 1)
    for c in range(len(heads) // 2):
        o_ref[:, c * LANES:(c + 1) * LANES] = jnp.where(
            lane < HEAD_DIM, outs[2 * c], outs[2 * c + 1]).astype(BF16)


def _att_win(sink, qt, kk, vvt, kx, vxt, *, nctx, dec_batch, dec_seq):
    tq = ATT_A_TQ
    assert tq == WINDOW
    nblk = dec_seq // tq
    base = nctx // tq
    past = kx.shape[2]

    def blk(b, i, delta):
        return base + b * nblk + jnp.clip(i + delta, 0, nblk - 1)

    k_specs = [pl.BlockSpec((KV_HEADS, tq, LANES), lambda b, i, d=d: (0, blk(b, i, d), 0)) for d in (-1, 0, 1)]
    vt_specs = [pl.BlockSpec((KV_HEADS, LANES, tq), lambda b, i, d=d: (0, 0, blk(b, i, d))) for d in (-1, 0, 1)]
    return pl.pallas_call(
        functools.partial(_att_win_kernel, nblk=nblk),
        grid=(dec_batch, nblk),
        in_specs=[pl.BlockSpec(memory_space=pltpu.SMEM),
                  pl.BlockSpec((KV_HEADS, Q_GROUP, LANES, tq), lambda b, i: (0, 0, 0, blk(b, i, 0)))]
                 + k_specs + vt_specs
                 + [pl.BlockSpec((None, KV_HEADS, past, LANES), lambda b, i: (b, 0, 0, 0)),
                    pl.BlockSpec((None, KV_HEADS, LANES, past), lambda b, i: (b, 0, 0, 0))],
        out_specs=pl.BlockSpec((tq, Q_WIDTH), lambda b, i: (b * nblk + i, 0)),
        out_shape=jax.ShapeDtypeStruct((dec_batch * dec_seq, Q_WIDTH), BF16),
        compiler_params=_cparams(("parallel", "parallel")),
        name="att_win",
    )(sink, qt, kk, kk, kk, vvt, vvt, vvt, kx, vxt)


def _att_dense_kernel(qt_ref, k_ref, vt_ref, o_ref, m_scr, acc_scr):
    kc = pl.program_id(2)
    tq = qt_ref.shape[3]

    @pl.when(kc == 0)
    def _():
        m_scr[...] = jnp.full(m_scr.shape, -jnp.inf, F32)
        acc_scr[...] = jnp.zeros(acc_scr.shape, F32)

    scores = [_dot(k_ref[j], qt_ref[j, g]) for j in range(KV_HEADS) for g in range(Q_GROUP)]
    for j in range(KV_HEADS):
        for g in range(Q_GROUP):
            h = j * Q_GROUP + g
            st = scores[h]
            m_prev = m_scr[h]
            m_new = jnp.maximum(m_prev, jnp.max(st, axis=0, keepdims=True))
            alpha = jnp.exp(m_prev - m_new)
            p = jnp.exp(st - m_new)
            acc_scr[h] = alpha * acc_scr[h] + _dot(vt_ref[g % 2, j], p.astype(BF16))
            m_scr[h] = m_new

    @pl.when(kc == pl.num_programs(2) - 1)
    def _():
        lane = lax.broadcasted_iota(jnp.int32, (tq, LANES), 1)
        for c in range(KV_HEADS * Q_GROUP // 2):
            even, odd = acc_scr[2 * c], acc_scr[2 * c + 1]
            even = (even / even[HEAD_DIM:HEAD_DIM + 1]).T
            odd = (odd / odd[0:1]).T
            o_ref[:, c * LANES:(c + 1) * LANES] = jnp.where(lane < HEAD_DIM, even, odd).astype(BF16)


def _att_dense(qt, k_all, vt_all, *, nctx, dec_batch, dec_seq):
    tq = ATT_B_TQ
    nblk = dec_seq // tq
    base = nctx // tq
    nkeys = k_all.shape[2]
    tk = next(t for t in ATT_B_TK if nkeys % t == 0)
    nheads = KV_HEADS * Q_GROUP
    return pl.pallas_call(
        _att_dense_kernel,
        grid=(dec_batch, nblk, nkeys // tk),
        in_specs=[pl.BlockSpec((KV_HEADS, Q_GROUP, LANES, tq),
                               lambda b, i, k: (0, 0, 0, base + b * nblk + i)),
                  pl.BlockSpec((None, KV_HEADS, tk, LANES), lambda b, i, k: (b, 0, k, 0)),
                  pl.BlockSpec((None, 2, KV_HEADS, LANES, tk), lambda b, i, k: (b, 0, 0, 0, k))],
        out_specs=pl.BlockSpec((tq, Q_WIDTH), lambda b, i, k: (b * nblk + i, 0)),
        out_shape=jax.ShapeDtypeStruct((dec_batch * dec_seq, Q_WIDTH), BF16),
        scratch_shapes=[pltpu.VMEM((nheads, 1, tq), F32), pltpu.VMEM((nheads, LANES, tq), F32)],
        compiler_params=_cparams(("parallel", "parallel", "arbitrary")),
        name="att_dense",
    )(qt, k_all, vt_all)


def _lru_kernel(xf_ref, xfp_ref, xfn_ref, xb_ref, xbp_ref, xbn_ref, cw_ref, cb_ref,
                wa_hi_ref, wa_lo_ref, wx_hi_ref, wx_lo_ref, ba_ref, bx_ref, lam_ref,
                h0f_ref, h0b_ref, hf_ref, hb_ref,
                af_scr, bf_scr, ab_scr, bb_scr, sf_scr, sb_scr, *, nctx_blk, seq_blk):
    i = pl.program_id(0)
    nchunk = pl.num_programs(0)
    ts = xf_ref.shape[0]
    row = lax.broadcasted_iota(jnp.int32, (ts, LRU_WIDTH), 0)
    tile_row = lax.broadcasted_iota(jnp.int32, (SUBLANES, LRU_WIDTH), 0)

    def chunk_info(c):
        lat = c >= nctx_blk
        pos = (c - nctx_blk) % seq_blk
        starts = jnp.logical_or(jnp.logical_not(lat), pos == 0)
        ends = jnp.logical_or(jnp.logical_not(lat), pos == seq_blk - 1)
        group = jnp.where(lat, 1 + (c - nctx_blk) // seq_blk, 0)
        return starts, ends, group

    def gates(x_ref, xp_ref, xn_ref, starts, ends, d, a_scr, b_scr):
        x = x_ref[...]
        prev = jnp.where(starts, 0.0, xp_ref[SUBLANES - 1:SUBLANES, :])
        nxt0 = jnp.where(ends, 0.0, xn_ref[0:1, :])
        nxt1 = jnp.where(ends, 0.0, xn_ref[1:2, :])
        x_m1 = jnp.where(row == 0, prev, pltpu.roll(x, 1, 0))
        x_p1 = jnp.where(row == ts - 1, nxt0, pltpu.roll(x, ts - 1, 0))
        x_p2 = jnp.where(row == ts - 2, nxt0, jnp.where(row == ts - 1, nxt1, pltpu.roll(x, ts - 2, 0)))
        xc = (cw_ref[0:1, :] * x_m1 + cw_ref[1:2, :] * x + cw_ref[2:3, :] * x_p1
              + cw_ref[3:4, :] * x_p2 + cb_ref[...])
        xc_hi, xc_lo = _split(xc)

        def blockdiag(hi_ref, lo_ref):
            return _dot(xc_hi, hi_ref[d]) + _dot(xc_lo, hi_ref[d]) + _dot(xc_hi, lo_ref[d])

        r = _sigmoid(blockdiag(wa_hi_ref, wa_lo_ref) + ba_ref[d])
        g = _sigmoid(blockdiag(wx_hi_ref, wx_lo_ref) + bx_ref[d])
        neg_lam = -lam_ref[d]
        softplus = jnp.maximum(neg_lam, 0.0) + jnp.log1p(jnp.exp(-jnp.abs(neg_lam)))
        log_a = -LRU_C * r * softplus
        a = jnp.exp(log_a)
        a_scr[...] = a
        b_scr[...] = jnp.sqrt(1.0 - a * a) * (g * xc)

    f_starts, f_ends, f_group = chunk_info(i)
    cb = nchunk - 1 - i
    b_starts, b_ends, b_group = chunk_info(cb)
    gates(xf_ref, xfp_ref, xfn_ref, f_starts, f_ends, 0, af_scr, bf_scr)
    gates(xb_ref, xbp_ref, xbn_ref, b_starts, b_ends, 1, ab_scr, bb_scr)

    @pl.when(f_starts)
    def _():
        sf_scr[...] = jnp.broadcast_to(h0f_ref[pl.ds(f_group, 1), :], sf_scr.shape)

    @pl.when(b_ends)
    def _():
        sb_scr[...] = jnp.broadcast_to(h0b_ref[pl.ds(b_group, 1), :], sb_scr.shape)

    ntile = ts // SUBLANES

    def tile_step(k, carry):
        hf, hb = carry
        base_f = pl.multiple_of(k * SUBLANES, SUBLANES)
        base_b = pl.multiple_of((ntile - 1 - k) * SUBLANES, SUBLANES)
        a_f = af_scr[pl.ds(base_f, SUBLANES), :]
        b_f = bf_scr[pl.ds(base_f, SUBLANES), :]
        a_b = ab_scr[pl.ds(base_b, SUBLANES), :]
        b_b = bb_scr[pl.ds(base_b, SUBLANES), :]
        out_f = jnp.zeros((SUBLANES, LRU_WIDTH), F32)
        out_b = jnp.zeros((SUBLANES, LRU_WIDTH), F32)
        for r in range(SUBLANES):
            rb = SUBLANES - 1 - r
            hf = a_f[r:r + 1, :] * hf + b_f[r:r + 1, :]
            hb = a_b[rb:rb + 1, :] * hb + b_b[rb:rb + 1, :]
            out_f = jnp.where(tile_row == r, hf, out_f)
            out_b = jnp.where(tile_row == rb, hb, out_b)
        hf_ref[pl.ds(base_f, SUBLANES), :] = out_f
        hb_ref[pl.ds(base_b, SUBLANES), :] = out_b
        return hf, hb

    hf, hb = lax.fori_loop(0, ntile, tile_step, (sf_scr[0:1, :], sb_scr[0:1, :]))
    sf_scr[...] = jnp.broadcast_to(hf, sf_scr.shape)
    sb_scr[...] = jnp.broadcast_to(hb, sb_scr.shape)


def _lru(z1, conv_w, conv_b, wa_hi, wa_lo, wx_hi, wx_lo, ba, bx, lam, h0f, h0b, *, nctx, dec_seq):
    n = z1.shape[0]
    ts = TOKEN_TILE
    nchunk = n // ts
    xcol = COL_XR // LRU_WIDTH
    tiles = ts // SUBLANES
    last_tile = n // SUBLANES - 1

    def cur(rev):
        return lambda i: ((nchunk - 1 - i) if rev else i, xcol)

    def prev(rev):
        return lambda i: (jnp.maximum(((nchunk - 1 - i) if rev else i) * tiles - 1, 0), xcol)

    def nxt(rev):
        return lambda i: (jnp.minimum((((nchunk - 1 - i) if rev else i) + 1) * tiles, last_tile), xcol)

    x_specs = []
    for rev in (False, True):
        x_specs += [pl.BlockSpec((ts, LRU_WIDTH), cur(rev)),
                    pl.BlockSpec((SUBLANES, LRU_WIDTH), prev(rev)),
                    pl.BlockSpec((SUBLANES, LRU_WIDTH), nxt(rev))]
    full = lambda shape: pl.BlockSpec(shape, lambda i: (0,) * len(shape))
    w_spec = full((2, LRU_WIDTH, LRU_WIDTH))
    v_spec = full((2, 1, LRU_WIDTH))
    out_shape = jax.ShapeDtypeStruct((n, LRU_WIDTH), F32)
    scr = pltpu.VMEM((ts, LRU_WIDTH), F32)
    state = pltpu.VMEM((SUBLANES, LRU_WIDTH), F32)
    return pl.pallas_call(
        functools.partial(_lru_kernel, nctx_blk=nctx // ts, seq_blk=dec_seq // ts),
        grid=(nchunk,),
        in_specs=x_specs + [full((4, LRU_WIDTH)), full((1, LRU_WIDTH)),
                            w_spec, w_spec, w_spec, w_spec, v_spec, v_spec, v_spec,
                            full((SUBLANES, LRU_WIDTH)), full((SUBLANES, LRU_WIDTH))],
        out_specs=[pl.BlockSpec((ts, LRU_WIDTH), lambda i: (i, 0)),
                   pl.BlockSpec((ts, LRU_WIDTH), lambda i: (nchunk - 1 - i, 0))],
        out_shape=[out_shape, out_shape],
        scratch_shapes=[scr, scr, scr, scr, state, state],
        compiler_params=_cparams(("arbitrary",)),
        name="lru",
    )(z1, z1, z1, z1, z1, z1, conv_w, conv_b, wa_hi, wa_lo, wx_hi, wx_lo, ba, bx, lam, h0f, h0b)


def _merge_kernel(x_ref, mod_ref, oa_ctx_ref, oa_lat_ref, ob_ctx_ref, ob_lat_ref, hf_ref, hb_ref, z2_ref,
                  woa_ref, wob_ref, woc_ref, wout_ref, o_ref, *, nctx_blk):
    d = x_ref.shape[1]
    is_ctx = pl.program_id(0) < nctx_blk
    oa = jnp.where(is_ctx, oa_ctx_ref[...], oa_lat_ref[...])
    ob = jnp.where(is_ctx, ob_ctx_ref[...], ob_lat_ref[...])
    oc = (hf_ref[...] + hb_ref[...]) * _gelu(z2_ref[:, 0:LRU_WIDTH])
    ga = z2_ref[:, LRU_WIDTH:LRU_WIDTH + d]
    gb = z2_ref[:, LRU_WIDTH + d:LRU_WIDTH + 2 * d]
    gc = z2_ref[:, LRU_WIDTH + 2 * d:LRU_WIDTH + 3 * d]
    merged = (_sigmoid(ga) * _dot(oa, woa_ref[...])
              + _sigmoid(gb) * _dot(ob, wob_ref[...])
              + _sigmoid(gc) * _dot(oc.astype(BF16), woc_ref[...]))
    out = _dot(merged.astype(BF16), wout_ref[...])
    gate = mod_ref[0][:, 2 * d:3 * d]
    o_ref[...] = x_ref[...] + gate * out


def _merge(x, mod3, oa_ctx, oa_lat, ob_ctx, ob_lat, hf, hb, z2, wo_a, wo_b, wo_c, w_out, *, nctx, dec_seq):
    n, d = x.shape
    tm = TOKEN_TILE
    nctx_blk = nctx // tm
    row = lambda w: pl.BlockSpec((tm, w), lambda i: (i, 0))
    ctx_row = pl.BlockSpec((tm, Q_WIDTH), lambda i: (jnp.minimum(i, nctx_blk - 1), 0))
    lat_row = pl.BlockSpec((tm, Q_WIDTH), lambda i: (jnp.maximum(i - nctx_blk, 0), 0))
    full = lambda a: pl.BlockSpec(a.shape, lambda i: (0, 0))
    return pl.pallas_call(
        functools.partial(_merge_kernel, nctx_blk=nctx_blk),
        grid=(n // tm,),
        in_specs=[row(d),
                  pl.BlockSpec((1, 1, mod3.shape[2]),
                               lambda i: (_group_of_block(i, tm, nctx, dec_seq), 0, 0)),
                  ctx_row, lat_row, ctx_row, lat_row, row(LRU_WIDTH), row(LRU_WIDTH), row(Z2_WIDTH),
                  full(wo_a), full(wo_b), full(wo_c), full(w_out)],
        out_specs=row(d),
        out_shape=jax.ShapeDtypeStruct((n, d), F32),
        compiler_params=_cparams(("parallel",)),
        name="merge",
    )(x, mod3, oa_ctx, oa_lat, ob_ctx, ob_lat, hf, hb, z2, wo_a, wo_b, wo_c, w_out)


ORDER_SENTINEL = 1e9


def _candidate_groups():
    k, s = PEER_TOPK, SUBLANES
    split = 4
    groups, covered = [], set()
    for a in range(split):
        for b0 in range(0, k // (a + 1), s):
            groups.append((True, a, b0, 0))
            covered |= {(a, b) for b in range(b0, b0 + s)}
    for b in range(k // (split + 1)):
        for a0 in range(0, k // (b + 1), s):
            if a0 + s > split:
                groups.append((False, a0, b, max(split, a0)))
                covered |= {(a, b) for a in range(max(split, a0), a0 + s)}
    needed = {(a, b) for a in range(k) for b in range(k) if (a + 1) * (b + 1) <= k}
    assert needed <= covered and len(covered) == sum(
        s if f else a0 + s - amin for f, a0, _, amin in groups)
    return groups


class _TopRows:
    def __init__(self, s, extras=(), order=None):
        self.s = s
        self.rows = lax.broadcasted_iota(jnp.int32, s.shape, 0).astype(F32) if order is None else order
        self.extras = extras
        self.vals, self.idxs, self.picked = [], [], [[] for _ in extras]

    def round(self, anchor=None):
        m = jnp.max(self.s, axis=0, keepdims=True)
        if anchor is not None:
            m = jnp.maximum(m, anchor)
        pos = jnp.min(jnp.where(self.s == m, self.rows, ORDER_SENTINEL), axis=0, keepdims=True)
        sel = self.rows == pos
        self.vals.append(m)
        self.idxs.append(pos)
        for dst, ex in zip(self.picked, self.extras):
            dst.append(jnp.max(jnp.where(sel, ex, -1.0), axis=0, keepdims=True))
        self.s = jnp.where(sel, -jnp.inf, self.s)

    def run(self, k):
        for _ in range(k):
            self.round()
        return self

    def result(self):
        cat = lambda xs: jnp.concatenate(xs, axis=0)
        return cat(self.vals), cat(self.idxs), [cat(p) for p in self.picked]


def _subkey_tops(selectors):
    return [a for sel in selectors for a in sel.result()[:2]]


def _pair_gates(selector):
    best, _, (e1, e2) = selector.result()
    ex = jnp.exp(best - best[0:1])
    return ex / jnp.sum(ex, axis=0, keepdims=True), e1, e2


def _peer_topk_kernel(q_ref, khi_ref, klo_ref, g_ref, i1_ref, i2_ref):
    for t0 in range(0, q_ref.shape[0], LANES):
        subs = [_TopRows(s).run(PEER_TOPK) for s in _peer_scores(q_ref, khi_ref, klo_ref, t0)]
        picked = _pair_gates(_peer_pair_rows(*_subkey_tops(subs)).run(PEER_TOPK))
        g_ref[0, :, t0:t0 + LANES], i1_ref[0, :, t0:t0 + LANES], i2_ref[0, :, t0:t0 + LANES] = picked


def _peer_scores(q_ref, khi_ref, klo_ref, t0):
    scores = []
    for p in range(2):
        q_hi, q_lo = _split(q_ref[t0:t0 + LANES, p * PEER_DKEY:(p + 1) * PEER_DKEY])
        k_hi, k_lo = khi_ref[0, p], klo_ref[0, p]
        scores.append(_dot_nt(k_hi, q_hi) + _dot_nt(k_lo, q_hi) + _dot_nt(k_hi, q_lo))
    return scores


def _peer_pair_rows(vals0, idx0, vals1, idx1):
    top_s, top_i = (vals0, vals1), (idx0, idx1)
    tb = top_s[0].shape[1]
    gshape = (SUBLANES, tb)
    sub_row = lax.broadcasted_iota(jnp.int32, gshape, 0).astype(F32)
    cand_s, cand_1, cand_2, order = [], [], [], []
    for fixed_first, a, b, a_min in _candidate_groups():
        if fixed_first:
            bcast = lambda x: jnp.broadcast_to(x[a:a + 1], gshape)
            cand_s.append(bcast(top_s[0]) + top_s[1][b:b + SUBLANES])
            cand_1.append(bcast(top_i[0]))
            cand_2.append(top_i[1][b:b + SUBLANES])
            order.append(float(a * PEER_TOPK + b) + sub_row)
        else:
            bcast = lambda x: jnp.broadcast_to(x[b:b + 1], gshape)
            live = sub_row >= float(a_min - a)
            cand_s.append(jnp.where(live, top_s[0][a:a + SUBLANES] + bcast(top_s[1]), -jnp.inf))
            cand_1.append(top_i[0][a:a + SUBLANES])
            cand_2.append(bcast(top_i[1]))
            order.append(jnp.where(live, float(a * PEER_TOPK + b) + float(PEER_TOPK) * sub_row,
                                   ORDER_SENTINEL))
    cat = lambda xs: jnp.concatenate(xs, axis=0)
    return _TopRows(cat(cand_s), (cat(cand_1), cat(cand_2)), cat(order))


def _peer_topk(q, keys_hi, keys_lo):
    n = q.shape[0]
    tb = PEER_TB
    spec = pl.BlockSpec((1, PEER_TOPK, tb), lambda i, h: (h, 0, i))
    shape = jax.ShapeDtypeStruct((PEER_HEADS, PEER_TOPK, n), F32)
    k_spec = pl.BlockSpec((1, 2, PEER_NKEYS, PEER_DKEY), lambda i, h: (h, 0, 0, 0))
    return pl.pallas_call(
        _peer_topk_kernel,
        grid=(n // tb, PEER_HEADS),
        in_specs=[pl.BlockSpec((tb, 2 * PEER_DKEY), lambda i, h: (i, h)), k_spec, k_spec],
        out_specs=[spec, spec, spec],
        out_shape=[shape, shape, shape],
        compiler_params=_cparams(("parallel", "parallel")),
        name="peer_topk",
    )(q, keys_hi, keys_lo)


def _peer_expert_kernel(x_ref, mod_ref, gn_ref, first_g_ref, first_i1_ref, first_i2_ref,
                        qn_ref, khi_ref, klo_ref, ut_ref, v_ref, o_ref,
                        h_scr, w_scr, p_scr, acc_scr, sub_scr, pair_scr, tok_scr):
    i = pl.program_id(0)
    c = pl.program_id(1)
    tb, d = x_ref.shape
    ec = ut_ref.shape[1]
    sub = ec // PEER_NKEYS
    npair = PEER_HEADS * PEER_TOPK
    g_ref, i1_ref, i2_ref = tok_scr.at[0], tok_scr.at[1], tok_scr.at[2]

    tiles = range(0, tb, LANES)

    def pair_selectors():
        return [_peer_pair_rows(*[sub_scr[k, :, t0:t0 + LANES] for k in range(4)]) for t0 in tiles]

    def store_pairs(head, selectors):
        rows = pl.ds(pl.multiple_of(head * PEER_TOPK, PEER_TOPK), PEER_TOPK)
        for t0, sel in zip(tiles, selectors):
            for k, a in enumerate(_pair_gates(sel)):
                pair_scr[k, rows, t0:t0 + LANES] = a

    @pl.when(jnp.logical_and(c == 0, i == 0))
    def _():
        sub_scr[...] = jnp.zeros(sub_scr.shape, F32)
        for k, ref in enumerate((first_g_ref, first_i1_ref, first_i2_ref)):
            tok_scr[k] = ref[...].reshape(npair, tb).T

    @pl.when(jnp.logical_and(c == 0, i > 0))
    def _():
        store_pairs(PEER_HEADS - 1, [sel.run(PEER_TOPK) for sel in pair_selectors()])
        for k in range(3):
            tok_scr[k] = pair_scr[k].T

    @pl.when(c == 0)
    def _():
        x = x_ref[...]
        y = x * lax.rsqrt(jnp.mean(x * x, axis=-1, keepdims=True) + EPS) * gn_ref[...]
        m = mod_ref[0]
        h_scr[...] = (y * (1.0 + m[:, 4 * d:5 * d]) + m[:, 3 * d:4 * d]).astype(BF16)
        acc_scr[...] = jnp.zeros(acc_scr.shape, F32)
        key_row = lax.broadcasted_iota(jnp.int32, (PEER_NKEYS, npair), 0).astype(F32)

        def token_group(tg, carry):
            for u in range(PEER_TOKEN_UNROLL):
                t = tg * PEER_TOKEN_UNROLL + u
                gate = g_ref[pl.ds(t, 1), :]
                ga = jnp.where(key_row == i1_ref[pl.ds(t, 1), :], gate, 0.0).astype(BF16)
                ob = jnp.where(key_row == i2_ref[pl.ds(t, 1), :], 1.0, 0.0).astype(BF16)
                w_scr[pl.ds(pl.multiple_of(t * PEER_W_PITCH, SUBLANES), PEER_NKEYS), :] = _dot_nt(ga, ob)
            return carry

        lax.fori_loop(0, tb // PEER_TOKEN_UNROLL, token_group, 0)

    pair_sel = pair_selectors()
    sub_sel = [[_TopRows(s) for s in _peer_scores(qn_ref, khi_ref, klo_ref, t0)] for t0 in tiles]
    rounds = []
    for _ in range(PEER_TOPK):
        rounds.append([sel for two in sub_sel for sel in two])
        rounds.append(pair_sel)
    nblock = ec // PEER_BLOCK
    per_block = -(-len(rounds) // nblock)

    def advance(h_b):
        anchor = jnp.minimum(h_b[0:1, 0:LANES], -jnp.inf)
        for n, sels in enumerate(rounds[:per_block]):
            for sel in sels:
                sel.round(anchor if n == 0 else None)
        del rounds[:per_block]

    def activate(b, h_b):
        for s in range(b * PEER_BLOCK // PEER_NKEYS, (b + 1) * PEER_BLOCK // PEER_NKEYS):
            w = w_scr[pl.ds(c * sub + s, tb, stride=PEER_W_PITCH), :]
            lo = s * PEER_NKEYS - b * PEER_BLOCK
            x = h_b[:, lo:lo + PEER_NKEYS]
            t = jnp.tanh(_gelu_inner(x).astype(BF16))
            p_scr[:, s * PEER_NKEYS:(s + 1) * PEER_NKEYS] = (0.5 * w).astype(BF16) * (x.astype(BF16) * (1.0 + t))
        return _dot(p_scr[:, b * PEER_BLOCK:(b + 1) * PEER_BLOCK], v_ref[b * PEER_BLOCK:(b + 1) * PEER_BLOCK, :])

    acc = acc_scr[...]
    h_prev = None
    for b in range(nblock):
        h_b = _dot(h_scr[...], ut_ref[:, b * PEER_BLOCK:(b + 1) * PEER_BLOCK])
        advance(h_b if h_prev is None else h_prev)
        if h_prev is not None:
            acc = acc + activate(b - 1, h_prev)
        h_prev = h_b
    acc_scr[...] = acc + activate(nblock - 1, h_prev)
    assert not rounds

    store_pairs(jnp.where(c == 0, PEER_HEADS - 1, c - 1), pair_sel)
    for t0, two in zip(tiles, sub_sel):
        for k, a in enumerate(_subkey_tops(two)):
            sub_scr[k, :, t0:t0 + LANES] = a

    @pl.when(c == pl.num_programs(1) - 1)
    def _():
        o_ref[...] = x_ref[...] + mod_ref[0][:, 5 * d:6 * d] * acc_scr[...]


def _peer_expert(x, mod3, gain, q, keys_hi, keys_lo, ut, v, *, nctx, dec_seq):
    n, d = x.shape
    tb, ec = PEER_TB, PEER_EC
    nexp = v.shape[0]
    ntile = n // tb
    npair = PEER_HEADS * PEER_TOPK
    assert nexp // ec == PEER_HEADS
    first = _peer_topk(q[:tb], keys_hi, keys_lo)
    row = lambda w: pl.BlockSpec((tb, w), lambda i, c: (i, 0))
    first_spec = pl.BlockSpec((PEER_HEADS, PEER_TOPK, tb), lambda i, c: (0, 0, 0))
    k_spec = pl.BlockSpec((1, 2, PEER_NKEYS, PEER_DKEY), lambda i, c: (c, 0, 0, 0))
    return pl.pallas_call(
        _peer_expert_kernel,
        grid=(ntile, nexp // ec),
        in_specs=[row(d),
                  pl.BlockSpec((1, 1, mod3.shape[2]),
                               lambda i, c: (_group_of_block(i, tb, nctx, dec_seq), 0, 0)),
                  pl.BlockSpec((1, d), lambda i, c: (0, 0)),
                  first_spec, first_spec, first_spec,
                  pl.BlockSpec((tb, 2 * PEER_DKEY), lambda i, c: (jnp.minimum(i + 1, ntile - 1), c)),
                  k_spec, k_spec,
                  pl.BlockSpec((d, ec), lambda i, c: (0, c)),
                  pl.BlockSpec((ec, d), lambda i, c: (c, 0))],
        out_specs=row(d),
        out_shape=jax.ShapeDtypeStruct((n, d), F32),
        scratch_shapes=[pltpu.VMEM((tb, d), BF16),
                        pltpu.VMEM((tb * PEER_W_PITCH, PEER_NKEYS), F32),
                        pltpu.VMEM((tb, ec), BF16),
                        pltpu.VMEM((tb, d), F32),
                        pltpu.VMEM((4, PEER_TOPK, tb), F32),
                        pltpu.VMEM((3, npair, tb), F32),
                        pltpu.VMEM((3, tb, npair), F32)],
        compiler_params=_cparams(("arbitrary", "arbitrary")),
        name="peer_expert",
    )(x, mod3, gain.reshape(1, d), *first, q, keys_hi, keys_lo, ut, v)


def _final_norm_kernel(x_ref, g_ref, octx_ref, olat_ref, *, nctx_blk):
    i = pl.program_id(0)
    x = x_ref[...]
    y = x * lax.rsqrt(jnp.mean(x * x, axis=-1, keepdims=True) + EPS) * g_ref[...]

    @pl.when(i < nctx_blk)
    def _():
        octx_ref[...] = y

    @pl.when(i >= nctx_blk)
    def _():
        olat_ref[...] = y


def _final_norm(x, gain, *, nctx):
    n, d = x.shape
    tm = 512
    nctx_blk = nctx // tm
    return pl.pallas_call(
        functools.partial(_final_norm_kernel, nctx_blk=nctx_blk),
        grid=(n // tm,),
        in_specs=[pl.BlockSpec((tm, d), lambda i: (i, 0)), pl.BlockSpec((1, d), lambda i: (0, 0))],
        out_specs=[pl.BlockSpec((tm, d), lambda i: (jnp.minimum(i, nctx_blk - 1), 0)),
                   pl.BlockSpec((tm, d), lambda i: (jnp.maximum(i - nctx_blk, 0), 0))],
        out_shape=[jax.ShapeDtypeStruct((nctx, d), F32), jax.ShapeDtypeStruct((n - nctx, d), F32)],
        compiler_params=_cparams(("arbitrary",)),
        name="final_norm",
    )(x, gain.reshape(1, d))


def _rope_tables(dec_seq, tile):
    t = jnp.arange(dec_seq)
    n_freq = HEAD_DIM // 4
    inv = ROPE_BASE ** (-jnp.arange(n_freq, dtype=F32) / n_freq)
    ang = jnp.concatenate([(t // GRID_W).astype(F32)[:, None] * inv,
                           (t % GRID_W).astype(F32)[:, None] * inv], axis=-1)
    cos, sin = jnp.cos(ang), jnp.sin(ang)
    reps = LANES // HEAD_DIM
    cos_t = jnp.tile(jnp.concatenate([cos, cos], axis=-1), (1, reps))
    sin_t = jnp.tile(jnp.concatenate([-sin, sin], axis=-1), (1, reps))
    cos_t = jnp.concatenate([jnp.ones((tile, LANES), F32), cos_t], axis=0)
    sin_t = jnp.concatenate([jnp.zeros((tile, LANES), F32), sin_t], axis=0)
    return cos_t, sin_t


def _dup_cache(cache):
    c = jnp.transpose(cache, (0, 2, 1, 3))
    return jnp.concatenate([c, c], axis=-1).astype(BF16)


def _dense_blockdiag(w):
    dirs, nb, bw, _ = w.shape
    eye = jnp.eye(nb, dtype=w.dtype)
    return jnp.einsum('dncf,nm->dncmf', w, eye).reshape(dirs, nb * bw, nb * bw)


def _pad_rows(a, rows):
    return jnp.concatenate([a, jnp.zeros((rows - a.shape[0],) + a.shape[1:], a.dtype)], axis=0)


def kernel(x_prompt, x_sample, c, cache_wa_k, cache_wa_v, cache_ax_k, cache_ax_v, state_lru_fwd,
           state_lru_bwd, c_ctx, w_ada, b_ada, g_norm1, w_in, wa_sink, ax_q_gain, ax_k_gain, conv_w,
           conv_b, lru_wa, lru_ba, lru_wx, lru_bx, lru_lambda, wo_a, wo_b, wo_c, w_out, g_norm2,
           peer_wq, peer_keys, peer_u, peer_v, g_final):
    batch, seq, d = x_prompt.shape
    dec_batch, dec_seq, _ = x_sample.shape
    depth = w_in.shape[0]
    nctx = batch * seq
    nlat = dec_batch * dec_seq
    assert seq == TOKEN_TILE and dec_seq % TOKEN_TILE == 0 and 1 + dec_batch <= SUBLANES
    assert w_in.shape[2] == Z1_WIDTH + Z2_WIDTH and d == 1024
    sizes = dict(nctx=nctx, dec_seq=dec_seq)

    x = jnp.concatenate([x_prompt.reshape(nctx, d), x_sample.reshape(nlat, d)], axis=0)
    cvec = _pad_rows(jnp.concatenate([c_ctx[None, :], c], axis=0), SUBLANES)
    mods = _ada(cvec, w_ada, b_ada)

    cos_tab, sin_tab = _rope_tables(dec_seq, TOKEN_TILE)
    head_mean = jnp.kron(jnp.eye(LANES // HEAD_DIM, dtype=F32),
                         jnp.full((HEAD_DIM, HEAD_DIM), 1.0 / HEAD_DIM, F32)).astype(BF16)
    reps = LANES // HEAD_DIM

    new_wa_k, new_wa_v, new_ax_k, new_ax_v, new_hf, new_hb = [], [], [], [], [], []
    for l in range(depth):
        mod3 = mods[l].reshape(SUBLANES, 1, 6 * d)
        w_in_bf = w_in[l].astype(BF16)
        z1 = _modmm(x, mod3, g_norm1[l], w_in_bf[:, :Z1_WIDTH], None, shift_col=0, tn=Z1_WIDTH, **sizes)
        z2 = _modmm(x, mod3, g_norm1[l], w_in_bf[:, Z1_WIDTH:], None, shift_col=0, tn=Z2_WIDTH, **sizes)

        gq = jnp.tile(ax_q_gain[l], reps).reshape(1, LANES)
        gk = jnp.tile(ax_k_gain[l], reps).reshape(1, LANES)
        qsa, kka, vva, qsb, kkb, vvb, kbn, qta, qtb = _prep(z1, cos_tab, sin_tab, gq, gk, head_mean, **sizes)

        oa_ctx, ob_ctx = _att_ctx(wa_sink[l], qsa, kka, vva, qsb, kkb, vvb, nseq=batch, seq=seq)
        oa_lat = _att_win(wa_sink[l], qta, kka, jnp.swapaxes(vva, 1, 2), _dup_cache(cache_wa_k[:, l]),
                          jnp.swapaxes(_dup_cache(cache_wa_v[:, l]), 2, 3),
                          nctx=nctx, dec_batch=dec_batch, dec_seq=dec_seq)

        def with_cache(cache, cur):
            lat = cur[:, nctx:].reshape(KV_HEADS, dec_batch, dec_seq, LANES)
            return jnp.concatenate([_dup_cache(cache), jnp.transpose(lat, (1, 0, 2, 3))], axis=2)

        vt = jnp.swapaxes(with_cache(cache_ax_v[:, l], vvb), 2, 3)
        vt2 = jnp.stack([vt.at[:, :, HEAD_DIM].set(1.0), vt.at[:, :, 0].set(1.0)], axis=1)
        ob_lat = _att_dense(qtb, with_cache(cache_ax_k[:, l], kkb), vt2,
                            nctx=nctx, dec_batch=dec_batch, dec_seq=dec_seq)

        wa_hi, wa_lo = _split(_dense_blockdiag(lru_wa[l]))
        wx_hi, wx_lo = _split(_dense_blockdiag(lru_wx[l]))
        vec = lambda a: a.reshape(2, 1, LRU_WIDTH)
        h0f = _pad_rows(jnp.concatenate([jnp.zeros((1, LRU_WIDTH), F32), state_lru_fwd[:, l]], axis=0), SUBLANES)
        h0b = _pad_rows(jnp.concatenate([jnp.zeros((1, LRU_WIDTH), F32), state_lru_bwd[:, l]], axis=0), SUBLANES)
        hf, hb = _lru(z1, conv_w[l], conv_b[l].reshape(1, LRU_WIDTH), wa_hi, wa_lo, wx_hi, wx_lo,
                      vec(lru_ba[l]), vec(lru_bx[l]), vec(lru_lambda[l]), h0f, h0b, **sizes)

        x = _merge(x, mod3, oa_ctx, oa_lat, ob_ctx, ob_lat, hf, hb, z2, wo_a[l].astype(BF16),
                   wo_b[l].astype(BF16), wo_c[l].astype(BF16), w_out[l].astype(BF16), **sizes)

        wq_hi, wq_lo = _split(peer_wq[l])
        q = _modmm(x, mod3, g_norm2[l], wq_hi, wq_lo, shift_col=3 * d, tn=wq_hi.shape[1], **sizes)
        keys_hi, keys_lo = _split(peer_keys[l])
        x = _peer_expert(x, mod3, g_norm2[l], q, keys_hi, keys_lo,
                         peer_u[l].T.astype(BF16), peer_v[l].astype(BF16), **sizes)

        ctx4 = lambda a: a[:nctx].reshape(batch, seq, KV_HEADS, HEAD_DIM)
        new_wa_k.append(ctx4(z1[:, COL_KA:COL_KA + KV_WIDTH]))
        new_wa_v.append(ctx4(z1[:, COL_VA:COL_VA + KV_WIDTH]))
        new_ax_k.append(ctx4(kbn))
        new_ax_v.append(ctx4(z1[:, COL_VB:COL_VB + KV_WIDTH]))
        new_hf.append(hf[:nctx].reshape(batch, seq, LRU_WIDTH)[:, -1])
        new_hb.append(hb[:nctx].reshape(batch, seq, LRU_WIDTH)[:, 0])

    y_ctx, y_lat = _final_norm(x, g_final, nctx=nctx)
    stack = lambda xs: jnp.stack(xs, axis=1)
    return (y_ctx.reshape(batch, seq, d), y_lat.reshape(dec_batch, dec_seq, d),
            stack(new_wa_k), stack(new_wa_v), stack(new_ax_k), stack(new_ax_v),
            stack(new_hf), stack(new_hb))
```
